```python
import jax
import jax.numpy as jnp
from jax import lax
import numpy as np

D_MODEL = 4096
BATCH = 4
SEQ = 2048
DEPTH = 4
DEC_BATCH = 128
DEC_SEQ = 4
PAST_LEN = 8192
PAGE_SIZE = 128

N_META = 16
EPS = 1e-6
M_HEADS = 8
M_DH = 128
M_WIDTH = M_HEADS * M_DH
M_CHUNK = 64
C_WIDTH = 1024
CONV_K = 31
A_HEADS = 16
A_NOPE = 128
A_ROPE = 64
A_V = 128
A_WIDTH = A_HEADS * A_V
Q_RANK = 768
KV_RANK = 512
ROPE_THETA = 10000.0
ATTN_SCALE = (A_NOPE + A_ROPE) ** -0.5
ATTN_BLOCK = 128
MIX_WIDTH = M_WIDTH + C_WIDTH + A_WIDTH
P_HEADS = 8
N_KEYS = 128
N_EXPERTS = N_KEYS * N_KEYS
P_QDIM = 256
P_HALF = P_QDIM // 2
P_TOPK = 16
P_BLOCK = 64
IN_SIZES = (M_WIDTH, M_WIDTH, M_WIDTH, M_WIDTH, M_HEADS, M_HEADS, C_WIDTH, C_WIDTH, Q_RANK, KV_RANK, A_ROPE)
IN_WIDTH = sum(IN_SIZES)
IN_SPLITS = tuple(int(s) for s in np.cumsum(IN_SIZES)[:-1])

kernel_name = 'hymba_mlstm_conformer_mla_peer_step'


def _rmsnorm(x, g):
    xf = x.astype(jnp.float32)
    y = xf * lax.rsqrt(jnp.mean(xf * xf, axis=-1, keepdims=True) + EPS)
    return (y * g.astype(jnp.float32)).astype(x.dtype)


def _layernorm(x, g, b):
    xf = x.astype(jnp.float32)
    xc = xf - jnp.mean(xf, axis=-1, keepdims=True)
    var = jnp.mean(xc * xc, axis=-1, keepdims=True)
    return (xc * lax.rsqrt(var + EPS) * g.astype(jnp.float32) + b.astype(jnp.float32)).astype(x.dtype)


def _rope_angles(pos):
    inv = jnp.power(ROPE_THETA, -jnp.arange(0, A_ROPE, 2, dtype=jnp.float32) / A_ROPE)
    ang = pos.astype(jnp.float32)[:, None] * inv[None, :]
    return jnp.cos(ang), jnp.sin(ang)


def _apply_rope(x, cos, sin):
    xf = x.astype(jnp.float32)
    x1, x2 = xf[..., :A_ROPE // 2], xf[..., A_ROPE // 2:]
    return jnp.concatenate([x1 * cos - x2 * sin, x2 * cos + x1 * sin], axis=-1).astype(x.dtype)


def _project(xn, w, gb):
    z = jnp.einsum('btd,de->bte', xn, w)
    qm, km, vm, om, im, fm, ca, cg, cq, ckv, kr = jnp.split(z, IN_SPLITS, axis=-1)
    f32 = jnp.float32
    hs = qm.shape[:2] + (M_HEADS, M_DH)
    q = qm.reshape(hs).astype(f32)
    k = km.reshape(hs).astype(f32) * (M_DH ** -0.5)
    v = vm.reshape(hs).astype(f32)
    gbf = gb.astype(f32)
    ig = im.astype(f32) + gbf[:M_HEADS]
    lf = jax.nn.log_sigmoid(fm.astype(f32) + gbf[M_HEADS:])
    u = ca * jax.nn.sigmoid(cg)
    return (q, k, v, ig, lf), om, u, cq, ckv, kr


def _mlstm_chunk(carry, q, k, v, ig, lf):
    C, n, m = carry
    L = q.shape[1]
    b = jnp.swapaxes(jnp.cumsum(lf, axis=1), 1, 2)
    igt = jnp.swapaxes(ig, 1, 2)
    causal = jnp.tril(jnp.ones((L, L), dtype=bool))
    logd = jnp.where(causal, b[..., :, None] - b[..., None, :] + igt[..., None, :], -jnp.inf)
    inter = b + m[..., None]
    m_t = jnp.maximum(inter, jnp.max(logd, axis=-1))
    s = jnp.einsum('blhd,bshd->bhls', q, k) * jnp.exp(logd - m_t[..., None])
    w_inter = jnp.exp(inter - m_t)
    numer = (jnp.einsum('bhls,bshd->blhd', s, v)
             + jnp.swapaxes(w_inter, 1, 2)[..., None] * jnp.einsum('blhk,bhkv->blhv', q, C))
    qn = jnp.einsum('blhk,bhk->bhl', q, n)
    denom = jnp.maximum(jnp.abs(jnp.sum(s, axis=-1) + w_inter * qn), jnp.exp(-m_t))
    h = numer / jnp.swapaxes(denom, 1, 2)[..., None]
    m_new = m_t[..., -1]
    w_last = jnp.exp(b[..., -1:] - b + igt - m_new[..., None])
    f_last = jnp.exp(b[..., -1] + m - m_new)
    C_new = f_last[..., None, None] * C + jnp.einsum('bhs,bshk,bshv->bhkv', w_last, k, v)
    n_new = f_last[..., None] * n + jnp.einsum('bhs,bshk->bhk', w_last, k)
    return (C_new, n_new, m_new), h


def _mlstm_prompt(q, k, v, ig, lf):
    B = q.shape[0]
    f32 = jnp.float32
    carry = (jnp.zeros((B, M_HEADS, M_DH, M_DH), f32), jnp.zeros((B, M_HEADS, M_DH), f32),
             jnp.zeros((B, M_HEADS), f32))
    carry, h_meta = _mlstm_chunk(carry, q[:, :N_META], k[:, :N_META], v[:, :N_META],
                                 ig[:, :N_META], lf[:, :N_META])

    def chunks(a):
        a = a[:, N_META:]
        return jnp.swapaxes(a.reshape((B, -1, M_CHUNK) + a.shape[2:]), 0, 1)

    carry, h_rest = lax.scan(lambda c, xs: _mlstm_chunk(c, *xs), carry,
                             (chunks(q), chunks(k), chunks(v), chunks(ig), chunks(lf)))
    h_rest = jnp.swapaxes(h_rest, 0, 1).reshape((B, -1) + h_rest.shape[3:])
    return carry, jnp.concatenate([h_meta, h_rest], axis=1)


def _mlstm_output(h, om, g):
    hn = h * lax.rsqrt(jnp.mean(h * h, axis=-1, keepdims=True) + EPS)
    hn = hn.reshape(h.shape[:-2] + (M_WIDTH,)) * g.astype(jnp.float32)
    return (jax.nn.sigmoid(om.astype(jnp.float32)) * hn).astype(om.dtype)


def _conv_tail(u_full, w, b, g, beta):
    y = lax.conv_general_dilated(u_full, w[:, None, :], (1,), 'VALID',
                                 dimension_numbers=('NWC', 'WIO', 'NWC'),
                                 feature_group_count=C_WIDTH) + b
    return jax.nn.silu(_layernorm(y, g, beta))


def _mla_queries(cq, q_g, w_uq, w_uk, cos, sin):
    cq = _rmsnorm(cq, q_g)
    q = jnp.einsum('btr,rhd->bthd', cq, w_uq)
    q_rope = _apply_rope(q[..., A_NOPE:], cos[:, None, :], sin[:, None, :])
    q_lat = jnp.einsum('bthd,chd->bthc', q[..., :A_NOPE], w_uk)
    return q_lat, q_rope


def _mla_scores(q_lat, q_rope, ckv, kr):
    s = jnp.einsum('bqhc,btc->bhqt', q_lat, ckv) + jnp.einsum('bqhr,btr->bhqt', q_rope, kr)
    return s.astype(jnp.float32) * ATTN_SCALE


def _mla_prompt_attend(q_lat, q_rope, ckv, kr):
    B, T = ckv.shape[:2]
    nqb = -(-T // ATTN_BLOCK)
    pad = nqb * ATTN_BLOCK - T

    def blk(a):
        a = jnp.pad(a, ((0, 0), (0, pad), (0, 0), (0, 0)))
        return jnp.swapaxes(a.reshape((B, nqb, ATTN_BLOCK) + a.shape[2:]), 0, 1)

    kpos = jnp.arange(T)

    def one(args):
        ib, ql, qr = args
        qpos = ib * ATTN_BLOCK + jnp.arange(ATTN_BLOCK)
        s = jnp.where(kpos[None, :] <= qpos[:, None], _mla_scores(ql, qr, ckv, kr), -jnp.inf)
        p = jax.nn.softmax(s, axis=-1).astype(ckv.dtype)
        return jnp.einsum('bhqt,btc->bqhc', p, ckv)

    o = lax.map(one, (jnp.arange(nqb), blk(q_lat), blk(q_rope)))
    return jnp.swapaxes(o, 0, 1).reshape(B, nqb * ATTN_BLOCK, A_HEADS, KV_RANK)[:, :T]


def _mla_sample_attend(q_lat, q_rope, ckv_new, kr_new, cache_ckv, cache_krope, page_table, li):
    DB = page_table.shape[0]
    ckv_past = cache_ckv[li, page_table].reshape(DB, -1, KV_RANK)
    kr_past = cache_krope[li, page_table].reshape(DB, -1, A_ROPE)
    P = ckv_past.shape[1]
    L = ckv_new.shape[1]
    s_past = _mla_scores(q_lat, q_rope, ckv_past.astype(q_lat.dtype), kr_past.astype(q_rope.dtype))
    s_new = jnp.where(jnp.tril(jnp.ones((L, L), dtype=bool)), _mla_scores(q_lat, q_rope, ckv_new, kr_new), -jnp.inf)
    p = jax.nn.softmax(jnp.concatenate([s_past, s_new], axis=-1), axis=-1).astype(ckv_new.dtype)
    return (jnp.einsum('bhqt,btc->bqhc', p[..., :P], ckv_past.astype(ckv_new.dtype))
            + jnp.einsum('bhqt,btc->bqhc', p[..., P:], ckv_new))


def _merge(h_m, om, y_c, o_lat, w_uv, mlstm_g, w_out):
    a_m = _mlstm_output(h_m, om, mlstm_g)
    o_a = jnp.einsum('bqhc,chd->bqhd', o_lat, w_uv)
    o_a = o_a.reshape(o_a.shape[:2] + (A_WIDTH,))
    cat = jnp.concatenate([a_m, y_c.astype(a_m.dtype), o_a.astype(a_m.dtype)], axis=-1)
    return jnp.einsum('bte,ed->btd', cat, w_out)


def _peer(xn, wq, sub_keys, u_all, v_all, li):
    shp = xn.shape
    xt = xn.reshape(-1, D_MODEL)
    ntok = xt.shape[0]
    q = jnp.einsum('nd,de->ne', xt, wq).reshape(ntok, P_HEADS, 2, P_HALF)
    s = jnp.einsum('nhpd,pkd->nhpk', q, sub_keys).astype(jnp.float32)
    sv, si = lax.top_k(s, P_TOPK)
    cand = (sv[..., 0, :, None] + sv[..., 1, None, :]).reshape(ntok, P_HEADS, -1)
    cidx = (si[..., 0, :, None] * N_KEYS + si[..., 1, None, :]).reshape(ntok, P_HEADS, -1)
    top_v, top_pos = lax.top_k(cand, P_TOPK)
    eidx = jnp.take_along_axis(cidx, top_pos, axis=-1)
    g = jax.nn.softmax(top_v, axis=-1).astype(xn.dtype)
    nb = -(-ntok // P_BLOCK)
    pad = nb * P_BLOCK - ntok

    def blk(a):
        a = jnp.pad(a, [(0, pad)] + [(0, 0)] * (a.ndim - 1))
        return a.reshape((nb, P_BLOCK) + a.shape[1:])

    def one(args):
        xb, eb, gb = args
        hid = jnp.einsum('nd,nhkd->nhk', xb, u_all[li, eb])
        return jnp.einsum('nhk,nhkd->nd', jax.nn.gelu(hid) * gb, v_all[li, eb])

    out = lax.map(one, (blk(xt), blk(eidx), blk(g)))
    return out.reshape(-1, D_MODEL)[:ntok].reshape(shp)


def setup_inputs(seed: int = 0) -> dict:
    key = jax.random.key(seed)
    ks = jax.random.split(key, 32)
    f32 = jnp.float32
    n_pages = PAST_LEN // PAGE_SIZE
    n_used = DEC_BATCH * n_pages
    n_pool = n_used + (n_used + 3) // 4

    def nrm(k, shape, scale):
        return jax.random.normal(k, shape, f32) * scale

    def gain(k, shape):
        return 1.0 + 0.02 * jax.random.normal(k, shape, f32)

    def per_layer_normal(k, shape):
        return jax.vmap(lambda kk: jax.random.normal(kk, shape, f32))(jax.random.split(k, DEPTH))

    page_table = jax.random.permutation(ks[4], n_pool)[:n_used].reshape(DEC_BATCH, n_pages).astype(jnp.int32)
    gk1, gk2 = jax.random.split(ks[12])
    gate_b = jnp.concatenate([nrm(gk1, (DEPTH, M_HEADS), 0.1),
                              3.0 + nrm(gk2, (DEPTH, M_HEADS), 0.5)], axis=-1)
    return {
        'x_prompt': nrm(ks[0], (BATCH, SEQ, D_MODEL), 1.0),
        'x_sample': nrm(ks[1], (DEC_BATCH, DEC_SEQ, D_MODEL), 1.0),
        'cache_ckv': per_layer_normal(ks[2], (n_pool, PAGE_SIZE, KV_RANK)),
        'cache_krope': per_layer_normal(ks[3], (n_pool, PAGE_SIZE, A_ROPE)),
        'state_mlstm_C': nrm(ks[5], (DEPTH, DEC_BATCH, M_HEADS, M_DH, M_DH), 0.1),
        'state_mlstm_n': nrm(ks[6], (DEPTH, DEC_BATCH, M_HEADS, M_DH), 0.5),
        'state_mlstm_m': nrm(ks[7], (DEPTH, DEC_BATCH, M_HEADS), 1.0),
        'state_conv': nrm(ks[8], (DEPTH, DEC_BATCH, CONV_K - 1, C_WIDTH), 0.5),
        'page_table': page_table,
        'meta_tokens': nrm(ks[9], (N_META, D_MODEL), 1.0),
        'ln_mix_g': gain(ks[10], (DEPTH, D_MODEL)),
        'w_in': nrm(ks[11], (DEPTH, D_MODEL, IN_WIDTH), D_MODEL ** -0.5),
        'gate_b': gate_b,
        'mlstm_norm_g': gain(ks[13], (DEPTH, M_WIDTH)),
        'conv_w': nrm(ks[14], (DEPTH, CONV_K, C_WIDTH), CONV_K ** -0.5),
        'conv_b': nrm(ks[15], (DEPTH, C_WIDTH), 0.02),
        'conv_ln_g': gain(ks[16], (DEPTH, C_WIDTH)),
        'conv_ln_b': nrm(ks[17], (DEPTH, C_WIDTH), 0.02),
        'q_norm_g': gain(ks[18], (DEPTH, Q_RANK)),
        'kv_norm_g': gain(ks[19], (DEPTH, KV_RANK)),
        'w_uq': nrm(ks[20], (DEPTH, Q_RANK, A_HEADS, A_NOPE + A_ROPE), Q_RANK ** -0.5),
        'w_uk': nrm(ks[21], (DEPTH, KV_RANK, A_HEADS, A_NOPE), KV_RANK ** -0.5),
        'w_uv': nrm(ks[22], (DEPTH, KV_RANK, A_HEADS, A_V), KV_RANK ** -0.5),
        'w_out': nrm(ks[23], (DEPTH, MIX_WIDTH, D_MODEL), MIX_WIDTH ** -0.5),
        'ln_ffn_g': gain(ks[24], (DEPTH, D_MODEL)),
        'peer_wq': nrm(ks[25], (DEPTH, D_MODEL, P_HEADS * P_QDIM), D_MODEL ** -0.5),
        'peer_keys': nrm(ks[26], (DEPTH, 2, N_KEYS, P_HALF), P_HALF ** -0.5),
        'peer_u': nrm(ks[27], (DEPTH, N_EXPERTS, D_MODEL), D_MODEL ** -0.5),
        'peer_v': nrm(ks[28], (DEPTH, N_EXPERTS, D_MODEL), 0.5 * P_HEADS ** -0.5),
        'final_g': gain(ks[29], (D_MODEL,)),
    }


def reference(x_prompt, x_sample, cache_ckv, cache_krope, state_mlstm_C, state_mlstm_n, state_mlstm_m,
              state_conv, page_table, meta_tokens, ln_mix_g, w_in, gate_b, mlstm_norm_g, conv_w, conv_b,
              conv_ln_g, conv_ln_b, q_norm_g, kv_norm_g, w_uq, w_uk, w_uv, w_out, ln_ffn_g, peer_wq,
              peer_keys, peer_u, peer_v, final_g):
    f32 = jnp.float32
    B = x_prompt.shape[0]
    meta = jnp.broadcast_to(meta_tokens[None].astype(x_prompt.dtype), (B, N_META, D_MODEL))
    h_p = jnp.concatenate([meta, x_prompt], axis=1)
    h_s = x_sample
    T = h_p.shape[1]
    past_len = page_table.shape[1] * PAGE_SIZE
    cos_p, sin_p = _rope_angles(jnp.arange(T))
    cos_s, sin_s = _rope_angles(past_len + jnp.arange(h_s.shape[1]))
    out_p = [[] for _ in range(6)]
    out_s = [[] for _ in range(6)]
    for li in range(DEPTH):
        xn = _rmsnorm(h_p, ln_mix_g[li])
        (mq, mk, mv, mi, mf), om, u, cq, ckv, kr = _project(xn, w_in[li], gate_b[li])
        (C, n, m), h_m = _mlstm_prompt(mq, mk, mv, mi, mf)
        u_full = jnp.pad(u, ((0, 0), (CONV_K - 1, 0), (0, 0)))
        y_c = _conv_tail(u_full, conv_w[li], conv_b[li], conv_ln_g[li], conv_ln_b[li])
        q_lat, q_rope = _mla_queries(cq, q_norm_g[li], w_uq[li], w_uk[li], cos_p, sin_p)
        ckv_n = _rmsnorm(ckv, kv_norm_g[li])
        kr_r = _apply_rope(kr, cos_p, sin_p)
        o_lat = _mla_prompt_attend(q_lat, q_rope, ckv_n, kr_r)
        h_p = h_p + _merge(h_m, om, y_c, o_lat, w_uv[li], mlstm_norm_g[li], w_out[li])
        h_p = h_p + _peer(_rmsnorm(h_p, ln_ffn_g[li]), peer_wq[li], peer_keys[li], peer_u, peer_v, li)
        for lst, val in zip(out_p, (ckv_n, kr_r, C, n, m, u[:, -(CONV_K - 1):])):
            lst.append(val)
        xn = _rmsnorm(h_s, ln_mix_g[li])
        (mq, mk, mv, mi, mf), om, u, cq, ckv, kr = _project(xn, w_in[li], gate_b[li])
        carry0 = (state_mlstm_C[li].astype(f32), state_mlstm_n[li].astype(f32), state_mlstm_m[li].astype(f32))
        (C, n, m), h_m = _mlstm_chunk(carry0, mq, mk, mv, mi, mf)
        u_full = jnp.concatenate([state_conv[li].astype(u.dtype), u], axis=1)
        y_c = _conv_tail(u_full, conv_w[li], conv_b[li], conv_ln_g[li], conv_ln_b[li])
        q_lat, q_rope = _mla_queries(cq, q_norm_g[li], w_uq[li], w_uk[li], cos_s, sin_s)
        ckv_n = _rmsnorm(ckv, kv_norm_g[li])
        kr_r = _apply_rope(kr, cos_s, sin_s)
        o_lat = _mla_sample_attend(q_lat, q_rope, ckv_n, kr_r, cache_ckv, cache_krope, page_table, li)
        h_s = h_s + _merge(h_m, om, y_c, o_lat, w_uv[li], mlstm_norm_g[li], w_out[li])
        h_s = h_s + _peer(_rmsnorm(h_s, ln_ffn_g[li]), peer_wq[li], peer_keys[li], peer_u, peer_v, li)
        for lst, val in zip(out_s, (ckv_n, kr_r, C, n, m, u_full[:, -(CONV_K - 1):])):
            lst.append(val)
    y_prompt = _rmsnorm(h_p, final_g)[:, N_META:]
    y_sample = _rmsnorm(h_s, final_g)
    ckv_p, krope_p, C_p, n_p, m_p, conv_p = [jnp.stack(v) for v in out_p]
    ckv_s, krope_s, C_s, n_s, m_s, conv_s = [jnp.stack(v) for v in out_s]
    return (y_prompt, y_sample, ckv_p, krope_p, C_p, n_p, m_p, conv_p,
            ckv_s, krope_s, C_s, n_s, m_s, conv_s)
```

```python
import functools
import math

import jax
import jax.numpy as jnp
from jax import lax
from jax.experimental import pallas as pl
from jax.experimental.pallas import tpu as pltpu

F32 = jnp.float32
BF16 = jnp.bfloat16
EPS = 1e-6
ROPE_THETA = 10000.0
P_TOPK = 16
LANES = 128
SUBLANES = 8
VMEM_LIMIT = 56 * 1024 * 1024
NEG_INF = float("-inf")


def _cparams(n_axes):
    return pltpu.CompilerParams(dimension_semantics=("arbitrary",) * n_axes,
                                vmem_limit_bytes=VMEM_LIMIT)


def _pick(n, cands):
    for c in cands:
        if n % c == 0:
            return c
    raise ValueError(f"no tile for {n} in {cands}")


def _rmsnorm_body(x_ref, g_ref, *out_refs):
    x = x_ref[...]
    y = x * lax.rsqrt(jnp.mean(x * x, axis=-1, keepdims=True) + EPS) * g_ref[...]
    for o in out_refs:
        o[...] = y.astype(o.dtype)


def _rmsnorm(x, g, out_dtypes):
    m, d = x.shape
    tm = _pick(m, (256, 128, 64, 32, 16, 8))
    outs = pl.pallas_call(
        _rmsnorm_body,
        grid=(m // tm,),
        in_specs=[pl.BlockSpec((tm, d), lambda i: (i, 0)),
                  pl.BlockSpec((1, d), lambda i: (0, 0))],
        out_specs=[pl.BlockSpec((tm, d), lambda i: (i, 0)) for _ in out_dtypes],
        out_shape=[jax.ShapeDtypeStruct((m, d), dt) for dt in out_dtypes],
        compiler_params=_cparams(1),
        name="rmsnorm",
    )(x, g.reshape(1, d))
    return outs


def _mm_body(a_ref, b_ref, o_ref):
    o_ref[...] = jnp.dot(a_ref[...], b_ref[...], preferred_element_type=F32).astype(o_ref.dtype)


def _matmul(a, b, out_dtype):
    m, k = a.shape
    n = b.shape[1]
    tm = _pick(m, (512, 256, 128, 64, 32, 16, 8))
    tn = _pick(n, (768, 1024, 512, 640, 384, 256, 128))
    return pl.pallas_call(
        _mm_body,
        grid=(n // tn, m // tm),
        in_specs=[pl.BlockSpec((tm, k), lambda j, i: (i, 0)),
                  pl.BlockSpec((k, tn), lambda j, i: (0, j))],
        out_specs=pl.BlockSpec((tm, tn), lambda j, i: (i, j)),
        out_shape=jax.ShapeDtypeStruct((m, n), out_dtype),
        compiler_params=_cparams(2),
        name="matmul",
    )(a, b)


def _headmm_body(a_ref, w_ref, o_ref):
    o_ref[...] = jnp.dot(a_ref[...], w_ref[0], preferred_element_type=F32).astype(o_ref.dtype)


def _head_matmul(a, w, out_dtype):
    m = a.shape[0]
    h, k, n = w.shape
    return pl.pallas_call(
        _headmm_body,
        grid=(h,),
        in_specs=[pl.BlockSpec((m, k), lambda i: (0, i)),
                  pl.BlockSpec((1, k, n), lambda i: (i, 0, 0))],
        out_specs=pl.BlockSpec((m, n), lambda i: (0, i)),
        out_shape=jax.ShapeDtypeStruct((m, h * n), out_dtype),
        compiler_params=_cparams(1),
        name="head_matmul",
    )(a, w)


def _outproj_body(am_ref, yc_ref, oa_ref, w1_ref, w2_ref, w3_ref, res_ref, o_ref):
    acc = jnp.dot(am_ref[...], w1_ref[...], preferred_element_type=F32)
    acc += jnp.dot(yc_ref[...], w2_ref[...], preferred_element_type=F32)
    acc += jnp.dot(oa_ref[...], w3_ref[...], preferred_element_type=F32)
    o_ref[...] = res_ref[...] + acc


def _outproj(a_m, y_c, o_a, w_out, res):
    m, d = res.shape
    k1, k2, k3 = a_m.shape[1], y_c.shape[1], o_a.shape[1]
    assert k1 == k2 and k3 % k1 == 0
    tm = _pick(m, (512, 256, 128, 64, 32, 16, 8))
    tn = _pick(d, (512, 256, 128))
    return pl.pallas_call(
        _outproj_body,
        grid=(d // tn, m // tm),
        in_specs=[pl.BlockSpec((tm, k1), lambda j, i: (i, 0)),
                  pl.BlockSpec((tm, k2), lambda j, i: (i, 0)),
                  pl.BlockSpec((tm, k3), lambda j, i: (i, 0)),
                  pl.BlockSpec((k1, tn), lambda j, i: (0, j)),
                  pl.BlockSpec((k2, tn), lambda j, i: (1, j)),
                  pl.BlockSpec((k3, tn), lambda j, i: ((k1 + k2) // k3, j)),
                  pl.BlockSpec((tm, tn), lambda j, i: (i, j))],
        out_specs=pl.BlockSpec((tm, tn), lambda j, i: (i, j)),
        out_shape=jax.ShapeDtypeStruct((m, d), F32),
        compiler_params=_cparams(2),
        name="outproj",
    )(a_m, y_c, o_a, w_out, w_out, w_out, res)


def _split3(x):
    hi = x.astype(BF16)
    r1 = x - hi.astype(F32)
    mid = r1.astype(BF16)
    lo = (r1 - mid.astype(F32)).astype(BF16)
    return hi, mid, lo


def _dot_exact01(x, sel):
    hi, mid, lo = _split3(x)
    acc = jnp.dot(hi, sel, preferred_element_type=F32)
    acc += jnp.dot(mid, sel, preferred_element_type=F32)
    acc += jnp.dot(lo, sel, preferred_element_type=F32)
    return acc


def _dot01_left(sel, x):
    hi, mid, lo = _split3(x)
    acc = jnp.dot(sel, hi, preferred_element_type=F32)
    acc += jnp.dot(sel, mid, preferred_element_type=F32)
    acc += jnp.dot(sel, lo, preferred_element_type=F32)
    return acc


def _log_sigmoid(x):
    return jnp.minimum(x, 0.0) - jnp.log(1.0 + jnp.exp(-jnp.abs(x)))


def _mlstm_body(q_ref, k_ref, v_ref, o_ref, gt_ref, gb_ref, ng_ref, c0_ref, n0_ref, m0_ref,
                am_ref, cout_ref, nout_ref, mout_ref, c_sc, n_sc, m_sc,
                *, rows, heads, dh, t_valid, nchunk):
    LP = LANES
    c = pl.program_id(1)

    @pl.when(c == 0)
    def _():
        c_sc[...] = c0_ref[0]
        n_sc[...] = n0_ref[0]
        m_sc[...] = m0_ref[0]

    def pad_rows(x):
        if rows == LP:
            return x
        return jnp.concatenate([x, jnp.zeros((LP - rows, x.shape[1]), x.dtype)], axis=0)

    row_i = lax.broadcasted_iota(jnp.int32, (LP, LP), 0)
    col_i = lax.broadcasted_iota(jnp.int32, (LP, LP), 1)
    valid = ((c * rows + row_i) < t_valid) & (row_i < rows)
    x = pad_rows(gt_ref[...]) + gb_ref[...]
    ig = jnp.where(valid, x, NEG_INF)
    lf = jnp.where(valid, _log_sigmoid(x), 0.0)
    causal = col_i <= row_i
    tri = jnp.where(causal, 1.0, 0.0).astype(BF16)
    bc = _dot01_left(tri, lf)
    bt = bc.T
    igt = ig.T
    scale = dh ** -0.5
    for h in range(heads):
        hs = slice(h * dh, (h + 1) * dh)
        b_col = bc[:, heads + h:heads + h + 1]
        b_row = bt[heads + h:heads + h + 1, :]
        ig_row = igt[h:h + 1, :]
        ig_col = ig[:, h:h + 1]
        m_prev = m_sc[h:h + 1, 0:1]
        logd = jnp.where(causal, b_col - b_row + ig_row, NEG_INF)
        inter = b_col + m_prev
        m_t = jnp.maximum(inter, jnp.max(logd, axis=1, keepdims=True))
        qh = pad_rows(q_ref[:, hs])
        kh = pad_rows(k_ref[:, hs]) * scale
        vh = pad_rows(v_ref[:, hs])
        qb = qh.astype(BF16)
        kb = kh.astype(BF16)
        qk = lax.dot_general(qb, kb, (((1,), (1,)), ((), ())), preferred_element_type=F32)
        s = qk * jnp.exp(logd - m_t)
        w_inter = jnp.exp(inter - m_t)
        ch = c_sc[h]
        numer = (jnp.dot(s.astype(BF16), vh.astype(BF16), preferred_element_type=F32)
                 + w_inter * jnp.dot(qb, ch.astype(BF16), preferred_element_type=F32))
        nh = n_sc[h:h + 1, :]
        qn = jnp.sum(qh * nh, axis=1, keepdims=True)
        denom = jnp.maximum(jnp.abs(jnp.sum(s, axis=1, keepdims=True) + w_inter * qn), jnp.exp(-m_t))
        hh = numer / denom
        hn = hh * lax.rsqrt(jnp.mean(hh * hh, axis=1, keepdims=True) + EPS) * ng_ref[:, hs]
        gate = jax.nn.sigmoid(o_ref[:, hs])
        am_ref[:, hs] = (gate * hn[:rows]).astype(am_ref.dtype)
        m_new = m_t[LP - 1:LP, :]
        b_last = b_col[LP - 1:LP, :]
        wl = jnp.exp(b_last - b_col + ig_col - m_new)
        f_last = jnp.exp(b_last + m_prev - m_new)
        kv = lax.dot_general(kb, (wl * vh).astype(BF16), (((0,), (0,)), ((), ())),
                             preferred_element_type=F32)
        c_sc[h] = f_last * ch + kv
        n_sc[h:h + 1, :] = f_last * nh + jnp.sum(wl * kh, axis=0, keepdims=True)
        m_sc[h:h + 1, :] = jnp.broadcast_to(m_new, (1, LANES))

    @pl.when(c == nchunk - 1)
    def _():
        cout_ref[0] = c_sc[...]
        nout_ref[0] = n_sc[...]
        mout_ref[0] = m_sc[...]


def _mlstm(z, col, gate_bias, norm_g, c0, n0, m0, *, nseq, rows_per_seq, rows, t_valid, heads, dh):
    mw = heads * dh
    nchunk = rows_per_seq // rows
    m_rows = z.shape[0]
    qcol, kcol, vcol, ocol, gcol = (col[n] for n in ("q", "k", "v", "o", "gates"))

    def zspec(off, width):
        blk = off // width
        return pl.BlockSpec((rows, width), lambda b, c: (b * nchunk + c, blk))

    state3 = lambda: pl.BlockSpec((1, heads, dh, dh), lambda b, c: (b, 0, 0, 0))
    state2 = lambda: pl.BlockSpec((1, heads, LANES), lambda b, c: (b, 0, 0))
    body = functools.partial(_mlstm_body, rows=rows, heads=heads, dh=dh, t_valid=t_valid, nchunk=nchunk)
    return pl.pallas_call(
        body,
        grid=(nseq, nchunk),
        in_specs=[zspec(qcol, mw), zspec(kcol, mw), zspec(vcol, mw), zspec(ocol, mw), zspec(gcol, LANES),
                  pl.BlockSpec((1, LANES), lambda b, c: (0, 0)),
                  pl.BlockSpec((1, mw), lambda b, c: (0, 0)),
                  state3(), state2(), state2()],
        out_specs=[pl.BlockSpec((rows, mw), lambda b, c: (b * nchunk + c, 0)),
                   state3(), state2(), state2()],
        out_shape=[jax.ShapeDtypeStruct((m_rows, mw), BF16),
                   jax.ShapeDtypeStruct((nseq, heads, dh, dh), F32),
                   jax.ShapeDtypeStruct((nseq, heads, LANES), F32),
                   jax.ShapeDtypeStruct((nseq, heads, LANES), F32)],
        scratch_shapes=[pltpu.VMEM((heads, dh, dh), F32),
                        pltpu.VMEM((heads, LANES), F32),
                        pltpu.VMEM((heads, LANES), F32)],
        compiler_params=_cparams(2),
        name="mlstm",
    )(z, z, z, z, z, gate_bias, norm_g, c0, n0, m0)


CONV_HIST = 32


def _conv_body(a_ref, g_ref, hist_ref, w_ref, b_ref, lg_ref, lb_ref, y_ref, tail_ref, ubuf,
               *, rows, ksize, t_valid, nblk):
    c = pl.program_id(1)
    kh = ksize - 1
    lead = CONV_HIST - kh

    @pl.when(c == 0)
    def _():
        ubuf[0:CONV_HIST, :] = jnp.zeros((CONV_HIST, ubuf.shape[1]), F32)
        ubuf[lead:CONV_HIST, :] = hist_ref[0]

    u = a_ref[...] * jax.nn.sigmoid(g_ref[...])
    ubuf[CONV_HIST:CONV_HIST + rows, :] = u
    acc = jnp.broadcast_to(b_ref[...], u.shape)
    for k in range(ksize):
        acc = acc + w_ref[k:k + 1, :] * ubuf[lead + k:lead + k + rows, :]
    mu = jnp.mean(acc, axis=-1, keepdims=True)
    xc = acc - mu
    var = jnp.mean(xc * xc, axis=-1, keepdims=True)
    y = xc * lax.rsqrt(var + EPS) * lg_ref[...] + lb_ref[...]
    y_ref[...] = (y * jax.nn.sigmoid(y)).astype(y_ref.dtype)

    last_blk = (t_valid - 1) // rows
    off = CONV_HIST + (t_valid - last_blk * rows) - kh

    @pl.when(c == last_blk)
    def _():
        tail_ref[0] = ubuf[off:off + kh, :]

    if nblk > 1:
        ubuf[0:CONV_HIST, :] = ubuf[rows:rows + CONV_HIST, :]


def _conv(z, col, hist, w, b, lg, lb, *, nseq, rows_per_seq, rows, t_valid):
    ksize, cw = w.shape
    nblk = rows_per_seq // rows
    m_rows = z.shape[0]
    ablk, gblk = col["ca"] // cw, col["cg"] // cw
    vec = lambda: pl.BlockSpec((1, cw), lambda s, c: (0, 0))
    body = functools.partial(_conv_body, rows=rows, ksize=ksize, t_valid=t_valid, nblk=nblk)
    return pl.pallas_call(
        body,
        grid=(nseq, nblk),
        in_specs=[pl.BlockSpec((rows, cw), lambda s, c: (s * nblk + c, ablk)),
                  pl.BlockSpec((rows, cw), lambda s, c: (s * nblk + c, gblk)),
                  pl.BlockSpec((1, ksize - 1, cw), lambda s, c: (s, 0, 0)),
                  pl.BlockSpec((ksize, cw), lambda s, c: (0, 0)),
                  vec(), vec(), vec()],
        out_specs=[pl.BlockSpec((rows, cw), lambda s, c: (s * nblk + c, 0)),
                   pl.BlockSpec((1, ksize - 1, cw), lambda s, c: (s, 0, 0))],
        out_shape=[jax.ShapeDtypeStruct((m_rows, cw), BF16),
                   jax.ShapeDtypeStruct((nseq, ksize - 1, cw), F32)],
        scratch_shapes=[pltpu.VMEM((CONV_HIST + rows, cw), F32)],
        compiler_params=_cparams(2),
        name="conv",
    )(z, z, hist, w, b.reshape(1, cw), lg.reshape(1, cw), lb.reshape(1, cw))


def _kvprep_body(ckv_ref, kr_ref, g_ref, tab_ref, ckvn_ref, ckvb_ref, krd_ref, krb_ref):
    x = ckv_ref[...]
    y = x * lax.rsqrt(jnp.mean(x * x, axis=-1, keepdims=True) + EPS) * g_ref[...]
    ckvn_ref[...] = y
    ckvb_ref[...] = y.astype(BF16)
    t = kr_ref[...] * tab_ref[...]
    kr = t + pltpu.roll(t, LANES // 2, 1)
    krd_ref[...] = kr
    krb_ref[...] = kr.astype(BF16)


def _kvprep(z, col, g, tab, rows_per_tab):
    m = z.shape[0]
    r = g.shape[0]
    tm = _pick(rows_per_tab, (256, 128, 64, 32, 16, 8)) if rows_per_tab < m else _pick(m, (256, 128, 64, 32, 16, 8))
    ntab = tab.shape[0] // tm
    cblk, kblk = col["ckv"] // r, col["kr"] // LANES
    return pl.pallas_call(
        _kvprep_body,
        grid=(m // tm,),
        in_specs=[pl.BlockSpec((tm, r), lambda i: (i, cblk)),
                  pl.BlockSpec((tm, LANES), lambda i: (i, kblk)),
                  pl.BlockSpec((1, r), lambda i: (0, 0)),
                  pl.BlockSpec((tm, LANES), lambda i: (i % ntab, 0))],
        out_specs=[pl.BlockSpec((tm, r), lambda i: (i, 0)),
                   pl.BlockSpec((tm, r), lambda i: (i, 0)),
                   pl.BlockSpec((tm, LANES), lambda i: (i, 0)),
                   pl.BlockSpec((tm, LANES), lambda i: (i, 0))],
        out_shape=[jax.ShapeDtypeStruct((m, r), F32), jax.ShapeDtypeStruct((m, r), BF16),
                   jax.ShapeDtypeStruct((m, LANES), F32), jax.ShapeDtypeStruct((m, LANES), BF16)],
        compiler_params=_cparams(1),
        name="kv_prep",
    )(z, z, g.reshape(1, r), tab)


def _qprep_body(cq_ref, g_ref, w_ref, cos_ref, sin_ref, qn_ref, qr_ref, *, nope_w, rope_w, scale):
    x = cq_ref[...]
    y = (x * lax.rsqrt(jnp.mean(x * x, axis=-1, keepdims=True) + EPS) * g_ref[...]).astype(BF16)
    q = jnp.dot(y, w_ref[...], preferred_element_type=F32)
    qn_ref[...] = (q[:, :nope_w] * scale).astype(BF16)
    xr = q[:, nope_w:nope_w + rope_w]
    xrot = q[:, nope_w + rope_w:nope_w + 2 * rope_w]
    qr_ref[...] = ((xr * cos_ref[...] + xrot * sin_ref[...]) * scale).astype(BF16)


def _qprep(z, col, g, wq, cos_t, sin_t, *, nope_w, rope_w, scale):
    m = z.shape[0]
    r = g.shape[0]
    tm = _pick(min(m, cos_t.shape[0]), (256, 128, 64, 32, 16, 8))
    ntab = cos_t.shape[0] // tm
    cblk = col["cq"] // r
    body = functools.partial(_qprep_body, nope_w=nope_w, rope_w=rope_w, scale=scale)
    return pl.pallas_call(
        body,
        grid=(m // tm,),
        in_specs=[pl.BlockSpec((tm, r), lambda i: (i, cblk)),
                  pl.BlockSpec((1, r), lambda i: (0, 0)),
                  pl.BlockSpec(wq.shape, lambda i: (0, 0)),
                  pl.BlockSpec((tm, rope_w), lambda i: (i % ntab, 0)),
                  pl.BlockSpec((tm, rope_w), lambda i: (i % ntab, 0))],
        out_specs=[pl.BlockSpec((tm, nope_w), lambda i: (i, 0)),
                   pl.BlockSpec((tm, rope_w), lambda i: (i, 0))],
        out_shape=[jax.ShapeDtypeStruct((m, nope_w), BF16), jax.ShapeDtypeStruct((m, rope_w), BF16)],
        compiler_params=_cparams(1),
        name="q_prep",
    )(z, g.reshape(1, r), wq, cos_t, sin_t)


def _attn_body(qn_ref, qr_ref, kn_ref, kr_ref, v_ref, o_ref, *, tq, tk):
    h = pl.program_id(1)
    qi = pl.program_id(2)
    lane = lax.broadcasted_iota(jnp.int32, (tq, LANES), 1)
    odd = (h % 2) == 1
    keep = (lane >= LANES // 2) == odd
    qr = jnp.where(keep, qr_ref[0].astype(F32), 0.0).astype(BF16)
    q = jnp.concatenate([qn_ref[0], qr], axis=1)
    row = lax.broadcasted_iota(jnp.int32, (tq, tk), 0) + qi * tq
    colb = lax.broadcasted_iota(jnp.int32, (tq, tk), 1)

    def step(j, carry):
        m, l, acc = carry
        ks = pl.ds(pl.multiple_of(j * tk, tk), tk)
        k = jnp.concatenate([kn_ref[0, ks, :], kr_ref[0, ks, :]], axis=1)
        s = lax.dot_general(q, k, (((1,), (1,)), ((), ())), preferred_element_type=F32)
        s = jnp.where(colb + j * tk <= row, s, NEG_INF)
        m_new = jnp.maximum(m, jnp.max(s, axis=1, keepdims=True))
        p = jnp.exp(s - m_new)
        alpha = jnp.exp(m - m_new)
        l = alpha * l + jnp.sum(p, axis=1, keepdims=True)
        acc = alpha * acc + jnp.dot(p.astype(BF16), v_ref[0, ks, :], preferred_element_type=F32)
        return m_new, l, acc

    m0 = jnp.full((tq, 1), NEG_INF, F32)
    l0 = jnp.zeros((tq, 1), F32)
    a0 = jnp.zeros((tq, v_ref.shape[2]), F32)
    nkv = (qi * tq + tq + tk - 1) // tk
    m, l, acc = lax.fori_loop(0, nkv, step, (m0, l0, a0))
    o_ref[0] = (acc / l).astype(o_ref.dtype)


def _prompt_attention(qn, qr, kv, kr, *, nb, tp, heads):
    tq = tk = LANES
    nq = tp // tq
    dv = LANES
    qn3, qr3, kv3, kr3 = (a.reshape(nb, tp, a.shape[1]) for a in (qn, qr, kv, kr))
    body = functools.partial(_attn_body, tq=tq, tk=tk)
    out = pl.pallas_call(
        body,
        grid=(nb, heads, nq),
        in_specs=[pl.BlockSpec((1, tq, LANES), lambda b, h, i: (b, i, h)),
                  pl.BlockSpec((1, tq, LANES), lambda b, h, i: (b, i, h // 2)),
                  pl.BlockSpec((1, tp, LANES), lambda b, h, i: (b, 0, h)),
                  pl.BlockSpec((1, tp, LANES), lambda b, h, i: (b, 0, 0)),
                  pl.BlockSpec((1, tp, dv), lambda b, h, i: (b, 0, heads + h))],
        out_specs=pl.BlockSpec((1, tq, dv), lambda b, h, i: (b, i, h)),
        out_shape=jax.ShapeDtypeStruct((nb, tp, heads * dv), BF16),
        compiler_params=_cparams(3),
        name="prompt_attention",
    )(qn3, qr3, kv3, kr3, kv3)
    return out.reshape(nb * tp, heads * dv)


PAGES_PER_STEP = 8


def _sattn_body(pt_ref, ql_ref, qr_ref, cnew_ref, knew_ref, *rest, npg, page, n_new, heads, nsteps):
    ck_refs = rest[:npg]
    kr_refs = rest[npg:2 * npg]
    o_ref = rest[2 * npg]
    m_sc, l_sc, acc_sc = rest[2 * npg + 1:]
    j = pl.program_id(1)
    rope = kr_refs[0].shape[-1]

    @pl.when(j == 0)
    def _():
        m_sc[...] = jnp.full(m_sc.shape, NEG_INF, F32)
        l_sc[...] = jnp.zeros(l_sc.shape, F32)
        acc_sc[...] = jnp.zeros(acc_sc.shape, F32)

    q = jnp.concatenate([ql_ref[0], qr_ref[0]], axis=1)

    def keycat(ck, kr):
        zpad = jnp.zeros((ck.shape[0], LANES - rope), F32)
        return jnp.concatenate([ck, kr, zpad], axis=1).astype(BF16)

    def fold(s, vals):
        m_old = m_sc[...]
        m_new = jnp.maximum(m_old, jnp.max(s, axis=1, keepdims=True))
        p = jnp.exp(s - m_new)
        alpha = jnp.exp(m_old - m_new)
        l_sc[...] = alpha * l_sc[...] + jnp.sum(p, axis=1, keepdims=True)
        acc_sc[...] = alpha * acc_sc[...] + jnp.dot(p.astype(BF16), vals, preferred_element_type=F32)
        m_sc[...] = m_new

    for g in range(npg):
        ck = ck_refs[g][0, 0]
        k = keycat(ck, kr_refs[g][0, 0])
        s = lax.dot_general(q, k, (((1,), (1,)), ((), ())), preferred_element_type=F32)
        fold(s, ck.astype(BF16))

    @pl.when(j == nsteps - 1)
    def _():
        rows = q.shape[0]
        padr = page - cnew_ref.shape[1]
        cn = jnp.concatenate([cnew_ref[0], jnp.zeros((padr, cnew_ref.shape[2]), F32)], axis=0)
        kn = jnp.concatenate([knew_ref[0][:, :rope], jnp.zeros((padr, rope), F32)], axis=0)
        s = lax.dot_general(q, keycat(cn, kn), (((1,), (1,)), ((), ())), preferred_element_type=F32)
        qtok = lax.broadcasted_iota(jnp.int32, (rows, page), 0) // heads
        ktok = lax.broadcasted_iota(jnp.int32, (rows, page), 1)
        s = jnp.where((ktok <= qtok) & (ktok < n_new), s, NEG_INF)
        fold(s, cn.astype(BF16))
        o_ref[0] = (acc_sc[...] / l_sc[...]).astype(o_ref.dtype)


def _sample_attention(ql, qr, cnew, knew, cache_ckv, cache_krope, page_table, li, *, heads, n_new):
    nseq, rows, rank = ql.shape
    npages = page_table.shape[1]
    page = cache_ckv.shape[2]
    rope = cache_krope.shape[3]
    npg = _pick(npages, (PAGES_PER_STEP, 4, 2, 1))
    nsteps = npages // npg

    def cspec(g, width):
        return pl.BlockSpec((1, 1, page, width), lambda b, j, pt: (li, pt[b, j * npg + g], 0, 0))

    body = functools.partial(_sattn_body, npg=npg, page=page, n_new=n_new, heads=heads, nsteps=nsteps)
    grid_spec = pltpu.PrefetchScalarGridSpec(
        num_scalar_prefetch=1,
        grid=(nseq, nsteps),
        in_specs=[pl.BlockSpec((1, rows, rank), lambda b, j, pt: (b, 0, 0)),
                  pl.BlockSpec((1, rows, LANES), lambda b, j, pt: (b, 0, 0)),
                  pl.BlockSpec((1, SUBLANES, rank), lambda b, j, pt: (b, 0, 0)),
                  pl.BlockSpec((1, SUBLANES, LANES), lambda b, j, pt: (b, 0, 0))]
                 + [cspec(g, rank) for g in range(npg)] + [cspec(g, rope) for g in range(npg)],
        out_specs=pl.BlockSpec((1, rows, rank), lambda b, j, pt: (b, 0, 0)),
        scratch_shapes=[pltpu.VMEM((rows, 1), F32), pltpu.VMEM((rows, 1), F32), pltpu.VMEM((rows, rank), F32)],
    )
    return pl.pallas_call(
        body,
        grid_spec=grid_spec,
        out_shape=jax.ShapeDtypeStruct((nseq, rows, rank), BF16),
        compiler_params=_cparams(2),
        name="sample_attention",
    )(page_table, ql, qr, cnew, knew, *([cache_ckv] * npg), *([cache_krope] * npg))


def _top16(s, lane_f, width):
    vals, poss = [], []
    for _ in range(P_TOPK):
        m = jnp.max(s, axis=1, keepdims=True)
        pos = jnp.min(jnp.where(s == m, lane_f, float(width)), axis=1, keepdims=True)
        vals.append(m)
        poss.append(pos)
        s = jnp.where(lane_f == pos, NEG_INF, s)
    return vals, poss


def _route_body(q_ref, keys_ref, e0_ref, e1_ref, eidx_ref, gate_ref, *, nkeys):
    tm = q_ref.shape[0]
    h = pl.program_id(1)
    lane_f = lax.broadcasted_iota(jnp.int32, (tm, LANES), 1).astype(F32)
    lane2_f = lax.broadcasted_iota(jnp.int32, (tm, P_TOPK * P_TOPK), 1).astype(F32)
    lane_i = lax.broadcasted_iota(jnp.int32, (tm, LANES), 1)

    @pl.when(h == 0)
    def _():
        eidx_ref[...] = jnp.zeros(eidx_ref.shape, eidx_ref.dtype)
        gate_ref[...] = jnp.zeros(gate_ref.shape, gate_ref.dtype)

    packed = []
    for p in range(2):
        qj = q_ref[:, p * LANES:(p + 1) * LANES].astype(BF16)
        s = lax.dot_general(qj, keys_ref[p], (((1,), (1,)), ((), ())), preferred_element_type=F32)
        vals, poss = _top16(s, lane_f, nkeys)
        sv = jnp.zeros((tm, LANES), F32)
        si = jnp.zeros((tm, LANES), F32)
        for k in range(P_TOPK):
            sv = jnp.where(lane_i == k, vals[k], sv)
            si = jnp.where(lane_i == k, poss[k], si)
        packed.append((sv, si))
    (sv0, si0), (sv1, si1) = packed
    cand = _dot_exact01(sv0, e0_ref[...]) + _dot_exact01(sv1, e1_ref[...])
    cidx = (jnp.dot(si0.astype(BF16), e0_ref[...], preferred_element_type=F32) * float(nkeys)
            + jnp.dot(si1.astype(BF16), e1_ref[...], preferred_element_type=F32))
    tv, tpos = _top16(cand, lane2_f, P_TOPK * P_TOPK)
    denom = jnp.zeros((tm, 1), F32)
    eidx = jnp.zeros((tm, LANES), F32)
    ghead = jnp.zeros((tm, LANES), F32)
    for k in range(P_TOPK):
        e = jnp.sum(jnp.where(lane2_f == tpos[k], cidx, 0.0), axis=1, keepdims=True)
        w = jnp.exp(tv[k] - tv[0])
        denom = denom + w
        sel = lane_i == (h * P_TOPK + k)
        eidx = jnp.where(sel, e, eidx)
        ghead = jnp.where(sel, w, ghead)
    eidx_ref[...] += eidx.astype(jnp.int32)
    gate_ref[...] += ghead / denom


def _route(q, keys_b, *, pheads):
    m = q.shape[0]
    nkeys = keys_b.shape[1]
    tm = _pick(m, (256, 128, 64, 32, 16, 8))
    c = jnp.arange(P_TOPK * P_TOPK)
    e0 = (c[None, :] // P_TOPK == jnp.arange(LANES)[:, None]).astype(BF16)
    e1 = (c[None, :] % P_TOPK == jnp.arange(LANES)[:, None]).astype(BF16)
    body = functools.partial(_route_body, nkeys=nkeys)
    return pl.pallas_call(
        body,
        grid=(m // tm, pheads),
        in_specs=[pl.BlockSpec((tm, 2 * LANES), lambda i, h: (i, h)),
                  pl.BlockSpec(keys_b.shape, lambda i, h: (0, 0, 0)),
                  pl.BlockSpec(e0.shape, lambda i, h: (0, 0)),
                  pl.BlockSpec(e1.shape, lambda i, h: (0, 0))],
        out_specs=[pl.BlockSpec((tm, LANES), lambda i, h: (i, 0)),
                   pl.BlockSpec((tm, LANES), lambda i, h: (i, 0))],
        out_shape=[jax.ShapeDtypeStruct((m, LANES), jnp.int32), jax.ShapeDtypeStruct((m, LANES), F32)],
        compiler_params=_cparams(2),
        name="peer_route",
    )(q, keys_b, e0, e1)


def _peer_body(eidx_ref, gate_ref, x_ref, h_ref, u_hbm, v_hbm, o_ref, ubuf, vbuf, hid_sc, sem,
               *, li, n_tok, nsel):
    d = x_ref.shape[-1]
    nchunk = nsel // SUBLANES

    def row_copies(t, r, slot):
        e = eidx_ref[0, t, r]
        cu = pltpu.make_async_copy(u_hbm.at[li, pl.ds(e, 1), :], ubuf.at[slot, pl.ds(r, 1), :], sem.at[0, slot])
        cv = pltpu.make_async_copy(v_hbm.at[li, pl.ds(e, 1), :], vbuf.at[slot, pl.ds(r, 1), :], sem.at[1, slot])
        return cu, cv

    def issue(t, slot):
        def body(r, carry):
            cu, cv = row_copies(t, r, slot)
            cu.start()
            cv.start()
            return carry
        lax.fori_loop(0, nsel, body, 0)

    def wait(slot):
        pltpu.make_async_copy(u_hbm.at[li, pl.ds(0, nsel), :], ubuf.at[slot], sem.at[0, slot]).wait()
        pltpu.make_async_copy(v_hbm.at[li, pl.ds(0, nsel), :], vbuf.at[slot], sem.at[1, slot]).wait()

    o_ref[...] = h_ref[...]
    issue(0, 0)
    eye = (lax.broadcasted_iota(jnp.int32, (nsel, LANES), 0)
           == lax.broadcasted_iota(jnp.int32, (nsel, LANES), 1))

    def token(t, carry):
        slot = t % 2

        @pl.when(t + 1 < n_tok)
        def _():
            issue(t + 1, 1 - slot)

        wait(slot)
        xrow = x_ref[0, pl.ds(t, 1), :]
        grow = gate_ref[0, pl.ds(t, 1), :]
        gcol = jnp.sum(jnp.where(eye, grow, 0.0), axis=1, keepdims=True)

        def hid_step(i, c2):
            rs = pl.ds(pl.multiple_of(i * SUBLANES, SUBLANES), SUBLANES)
            prod = ubuf[slot, rs, :] * xrow
            hid_sc[rs, :] = jnp.broadcast_to(jnp.sum(prod, axis=1, keepdims=True), (SUBLANES, LANES))
            return c2
        lax.fori_loop(0, nchunk, hid_step, 0)
        hid_sc[...] = jax.nn.gelu(hid_sc[...]) * gcol

        def out_step(i, acc):
            rs = pl.ds(pl.multiple_of(i * SUBLANES, SUBLANES), SUBLANES)
            return acc + vbuf[slot, rs, :] * hid_sc[rs, 0:1]
        acc = lax.fori_loop(0, nchunk, out_step, jnp.zeros((SUBLANES, d), F32))
        o_ref[0, pl.ds(t, 1), :] = h_ref[0, pl.ds(t, 1), :] + jnp.sum(acc, axis=0, keepdims=True)
        return carry

    lax.fori_loop(0, n_tok, token, 0)


def _peer_experts(eidx, gate, xn, hres, peer_u, peer_v, li, *, nseq, rows_per_seq, rows, n_blk, n_tok):
    d = xn.shape[1]
    nsel = eidx.shape[1]
    e3, g3 = eidx.reshape(nseq, rows_per_seq, nsel), gate.reshape(nseq, rows_per_seq, nsel)
    x3, h3 = xn.reshape(nseq, rows_per_seq, d), hres.reshape(nseq, rows_per_seq, d)
    body = functools.partial(_peer_body, li=li, n_tok=n_tok, nsel=nsel)
    out = pl.pallas_call(
        body,
        grid=(nseq, n_blk),
        in_specs=[pl.BlockSpec((1, rows, nsel), lambda s, c: (s, c, 0), memory_space=pltpu.SMEM),
                  pl.BlockSpec((1, rows, nsel), lambda s, c: (s, c, 0)),
                  pl.BlockSpec((1, rows, d), lambda s, c: (s, c, 0)),
                  pl.BlockSpec((1, rows, d), lambda s, c: (s, c, 0)),
                  pl.BlockSpec(memory_space=pl.ANY),
                  pl.BlockSpec(memory_space=pl.ANY)],
        out_specs=pl.BlockSpec((1, rows, d), lambda s, c: (s, c, 0)),
        out_shape=jax.ShapeDtypeStruct((nseq, rows_per_seq, d), F32),
        scratch_shapes=[pltpu.VMEM((2, nsel, d), F32), pltpu.VMEM((2, nsel, d), F32),
                        pltpu.VMEM((nsel, LANES), F32), pltpu.SemaphoreType.DMA((2, 2))],
        input_output_aliases={3: 0},
        compiler_params=_cparams(2),
        name="peer_experts",
    )(e3, g3, x3, h3, peer_u, peer_v)
    return out.reshape(nseq * rows_per_seq, d)


def _layout(mw, cw, kvr, qr):
    col, off = {}, 0
    for name, width in (("q", mw), ("k", mw), ("v", mw), ("o", mw), ("ca", cw), ("cg", cw),
                        ("ckv", kvr), ("kr", LANES), ("gates", LANES), ("cq", qr)):
        off = -(-off // width) * width
        col[name] = off
        off += width
    total = -(-off // LANES) * LANES
    return col, total


def _rot_half(w):
    half = w.shape[-1] // 2
    return jnp.concatenate([-w[..., half:], w[..., :half]], axis=-1)


def _rope_tables(pos, rope, heads):
    inv = jnp.power(ROPE_THETA, -jnp.arange(0, rope, 2, dtype=F32) / rope)
    ang = pos.astype(F32)[:, None] * inv[None, :]
    cos, sin = jnp.cos(ang), jnp.sin(ang)
    cos2, sin2 = jnp.concatenate([cos, cos], -1), jnp.concatenate([sin, sin], -1)
    ktab = jnp.concatenate([cos2, sin2], -1)
    return ktab, jnp.tile(cos2, (1, heads)), jnp.tile(sin2, (1, heads))


def kernel(x_prompt, x_sample, cache_ckv, cache_krope, state_mlstm_C, state_mlstm_n, state_mlstm_m,
           state_conv, page_table, meta_tokens, ln_mix_g, w_in, gate_b, mlstm_norm_g, conv_w, conv_b,
           conv_ln_g, conv_ln_b, q_norm_g, kv_norm_g, w_uq, w_uk, w_uv, w_out, ln_ffn_g, peer_wq,
           peer_keys, peer_u, peer_v, final_g):
    nb, seq, d = x_prompt.shape
    db, dseq, _ = x_sample.shape
    depth = w_in.shape[0]
    n_meta = meta_tokens.shape[0]
    mh, dh = state_mlstm_C.shape[2], state_mlstm_C.shape[3]
    mw = mh * dh
    ksize, cw = conv_w.shape[1], conv_w.shape[2]
    qrank, kvr = q_norm_g.shape[1], kv_norm_g.shape[1]
    ah, nope = w_uk.shape[2], w_uk.shape[3]
    rope = w_uq.shape[3] - nope
    av = w_uv.shape[3]
    page = cache_ckv.shape[2]
    past = page_table.shape[1] * page
    nkeys, phalf = peer_keys.shape[2], peer_keys.shape[3]
    pheads = peer_wq.shape[2] // (2 * phalf)
    assert dh == LANES and nope == LANES and av == LANES and 2 * rope == LANES and phalf == LANES
    assert nkeys == LANES and pheads * P_TOPK == LANES and ah % 2 == 0 and dseq <= SUBLANES
    assert mw == cw and (ah * av) % mw == 0 and ksize - 1 <= CONV_HIST
    attn_scale = (nope + rope) ** -0.5

    t_valid = n_meta + seq
    tp = -(-t_valid // LANES) * LANES
    srows = SUBLANES
    col, n_in = _layout(mw, cw, kvr, qrank)
    in_sizes = (mw, mw, mw, mw, mh, mh, cw, cw, qrank, kvr, rope)
    in_off = [0]
    for s in in_sizes:
        in_off.append(in_off[-1] + s)

    meta = jnp.broadcast_to(meta_tokens[None].astype(F32), (nb, n_meta, d))
    h_p = jnp.concatenate([meta, x_prompt, jnp.zeros((nb, tp - t_valid, d), F32)], axis=1).reshape(nb * tp, d)
    h_s = jnp.pad(x_sample, ((0, 0), (0, srows - dseq), (0, 0))).reshape(db * srows, d)

    ktab_p, cos_p, sin_p = _rope_tables(jnp.arange(tp), rope, ah)
    pos_s = past + jnp.arange(srows)
    stile = _pick(db * srows, (256, 128, 64, 32, 16, 8)) // srows
    ktab_s, cos_s, sin_s = (jnp.tile(t, (stile, 1)) for t in _rope_tables(pos_s, rope, ah))

    peer_rows = _pick(t_valid, (48, 64, 32, 16, 8))
    assert tp % SUBLANES == 0

    groups = (
        dict(h=h_p, nseq=nb, rps=tp, tval=t_valid, mrows=LANES, crows=LANES, ktab=ktab_p, cos=cos_p, sin=sin_p,
             tabrows=tp, prow=peer_rows, pblk=t_valid // peer_rows, ptok=peer_rows),
        dict(h=h_s, nseq=db, rps=srows, tval=dseq, mrows=srows, crows=srows, ktab=ktab_s, cos=cos_s, sin=sin_s,
             tabrows=stile * srows, prow=srows, pblk=1, ptok=dseq),
    )
    outs = [[[] for _ in range(6)] for _ in range(2)]

    for li in range(depth):
        w = w_in[li]
        seg = {n: w[:, in_off[i]:in_off[i + 1]] for i, n in enumerate(
            ("q", "k", "v", "o", "ig", "fg", "ca", "cg", "cq", "ckv", "kr"))}
        pieces = {"q": seg["q"], "k": seg["k"], "v": seg["v"], "o": seg["o"], "ca": seg["ca"], "cg": seg["cg"],
                  "ckv": seg["ckv"], "kr": jnp.concatenate([seg["kr"], _rot_half(seg["kr"])], axis=1),
                  "gates": jnp.concatenate([seg["ig"], seg["fg"]], axis=1), "cq": seg["cq"]}
        w_packed = jnp.zeros((d, n_in), BF16)
        for name, piece in pieces.items():
            w_packed = lax.dynamic_update_slice(w_packed, piece.astype(BF16), (0, col[name]))
        gbias = jnp.zeros((1, LANES), F32).at[0, :2 * mh].set(gate_b[li])
        wq_n = w_uq[li][:, :, :nope].reshape(qrank, ah * nope)
        wq_r = w_uq[li][:, :, nope:]
        wq_packed = jnp.concatenate([wq_n, wq_r.reshape(qrank, ah * rope),
                                     _rot_half(wq_r).reshape(qrank, ah * rope)], axis=1).astype(BF16)
        wkv = jnp.concatenate([w_uk[li].reshape(kvr, ah * nope), w_uv[li].reshape(kvr, ah * av)], axis=1).astype(BF16)
        w_ukt = jnp.transpose(w_uk[li], (1, 2, 0)).astype(BF16)
        w_uvh = jnp.transpose(w_uv[li], (1, 0, 2)).astype(BF16)
        w_out_b = w_out[li].astype(BF16)
        wpq = peer_wq[li].astype(BF16)
        keys_b = peer_keys[li].astype(BF16)

        for gi, g in enumerate(groups):
            h = g["h"]
            nseq, rps = g["nseq"], g["rps"]
            (xn,) = _rmsnorm(h, ln_mix_g[li], (BF16,))
            z = _matmul(xn, w_packed, F32)

            if gi == 0:
                c0 = jnp.zeros((nseq, mh, dh, dh), F32)
                n0 = jnp.zeros((nseq, mh, LANES), F32)
                m0 = jnp.zeros((nseq, mh, LANES), F32)
                hist = jnp.zeros((nseq, ksize - 1, cw), F32)
            else:
                c0 = state_mlstm_C[li].astype(F32)
                n0 = state_mlstm_n[li].astype(F32)
                m0 = jnp.broadcast_to(state_mlstm_m[li].astype(F32)[:, :, None], (nseq, mh, LANES))
                hist = state_conv[li].astype(F32)
            a_m, c_new, n_new, m_new = _mlstm(z, col, gbias, mlstm_norm_g[li].reshape(1, mw), c0, n0, m0,
                                              nseq=nseq, rows_per_seq=rps, rows=g["mrows"], t_valid=g["tval"],
                                              heads=mh, dh=dh)
            y_c, tail = _conv(z, col, hist, conv_w[li], conv_b[li], conv_ln_g[li], conv_ln_b[li],
                              nseq=nseq, rows_per_seq=rps, rows=g["crows"], t_valid=g["tval"])
            ckv_n, ckv_b, kr_d, kr_b = _kvprep(z, col, kv_norm_g[li], g["ktab"], g["tabrows"])
            qn, qr = _qprep(z, col, q_norm_g[li], wq_packed, g["cos"], g["sin"],
                            nope_w=ah * nope, rope_w=ah * rope, scale=attn_scale)
            if gi == 0:
                kv = _matmul(ckv_b, wkv, BF16)
                o_a = _prompt_attention(qn, qr, kv, kr_b, nb=nseq, tp=rps, heads=ah)
            else:
                rows = srows * ah
                q_lat = _head_matmul(qn, w_ukt, BF16).reshape(nseq, rows, kvr)
                q_rope = jnp.pad(qr.reshape(nseq, rows, rope), ((0, 0), (0, 0), (0, LANES - rope)))
                o_lat = _sample_attention(q_lat, q_rope, ckv_n.reshape(nseq, srows, kvr),
                                          kr_d.reshape(nseq, srows, LANES), cache_ckv, cache_krope,
                                          page_table, li, heads=ah, n_new=dseq)
                o_a = _head_matmul(o_lat.reshape(nseq * srows, ah * kvr), w_uvh, BF16)
            h = _outproj(a_m, y_c, o_a, w_out_b, h)

            xf, xb = _rmsnorm(h, ln_ffn_g[li], (F32, BF16))
            pq = _matmul(xb, wpq, F32)
            eidx, gate = _route(pq, keys_b, pheads=pheads)
            h = _peer_experts(eidx, gate, xf, h, peer_u, peer_v, li, nseq=nseq, rows_per_seq=rps,
                              rows=g["prow"], n_blk=g["pblk"], n_tok=g["ptok"])
            g["h"] = h

            tv = g["tval"]
            vals = (ckv_n.reshape(nseq, rps, kvr)[:, :tv], kr_d.reshape(nseq, rps, LANES)[:, :tv, :rope],
                    c_new, n_new[:, :, :dh], m_new[:, :, 0], tail)
            for lst, val in zip(outs[gi], vals):
                lst.append(val)

    (y_p,) = _rmsnorm(groups[0]["h"], final_g, (F32,))
    (y_s,) = _rmsnorm(groups[1]["h"], final_g, (F32,))
    y_prompt = y_p.reshape(nb, tp, d)[:, n_meta:t_valid]
    y_sample = y_s.reshape(db, srows, d)[:, :dseq]
    stacked_p = [jnp.stack(v) for v in outs[0]]
    stacked_s = [jnp.stack(v) for v in outs[1]]
    return (y_prompt, y_sample, *stacked_p, *stacked_s)
```

```python
import functools
import math

import jax
import jax.numpy as jnp
from jax import lax
from jax.experimental import pallas as pl
from jax.experimental.pallas import tpu as pltpu

F32 = jnp.float32
BF16 = jnp.bfloat16
EPS = 1e-6
ROPE_THETA = 10000.0
P_TOPK = 16
LANES = 128
SUBLANES = 8
VMEM_LIMIT = 56 * 1024 * 1024
NEG_INF = float("-inf")


def _cparams(n_axes):
    return pltpu.CompilerParams(dimension_semantics=("arbitrary",) * n_axes,
                                vmem_limit_bytes=VMEM_LIMIT)


def _pick(n, cands):
    for c in cands:
        if n % c == 0:
            return c
    raise ValueError(f"no tile for {n} in {cands}")


def _rmsnorm_body(x_ref, g_ref, *out_refs):
    x = x_ref[...]
    y = x * lax.rsqrt(jnp.mean(x * x, axis=-1, keepdims=True) + EPS) * g_ref[...]
    for o in out_refs:
        o[...] = y.astype(o.dtype)


def _rmsnorm(x, g, out_dtypes):
    m, d = x.shape
    tm = _pick(m, (256, 128, 64, 32, 16, 8))
    outs = pl.pallas_call(
        _rmsnorm_body,
        grid=(m // tm,),
        in_specs=[pl.BlockSpec((tm, d), lambda i: (i, 0)),
                  pl.BlockSpec((1, d), lambda i: (0, 0))],
        out_specs=[pl.BlockSpec((tm, d), lambda i: (i, 0)) for _ in out_dtypes],
        out_shape=[jax.ShapeDtypeStruct((m, d), dt) for dt in out_dtypes],
        compiler_params=_cparams(1),
        name="rmsnorm",
    )(x, g.reshape(1, d))
    return outs


def _mm_body(a_ref, b_ref, o_ref):
    o_ref[...] = jnp.dot(a_ref[...], b_ref[...], preferred_element_type=F32).astype(o_ref.dtype)


def _matmul(a, b, out_dtype):
    m, k = a.shape
    n = b.shape[1]
    tm = _pick(m, (512, 256, 128, 64, 32, 16, 8))
    tn = _pick(n, (768, 1024, 512, 640, 384, 256, 128))
    return pl.pallas_call(
        _mm_body,
        grid=(n // tn, m // tm),
        in_specs=[pl.BlockSpec((tm, k), lambda j, i: (i, 0)),
                  pl.BlockSpec((k, tn), lambda j, i: (0, j))],
        out_specs=pl.BlockSpec((tm, tn), lambda j, i: (i, j)),
        out_shape=jax.ShapeDtypeStruct((m, n), out_dtype),
        compiler_params=_cparams(2),
        name="matmul",
    )(a, b)


def _headmm_body(a_ref, w_ref, o_ref):
    o_ref[...] = jnp.dot(a_ref[...], w_ref[0], preferred_element_type=F32).astype(o_ref.dtype)


def _head_matmul(a, w, out_dtype):
    m = a.shape[0]
    h, k, n = w.shape
    return pl.pallas_call(
        _headmm_body,
        grid=(h,),
        in_specs=[pl.BlockSpec((m, k), lambda i: (0, i)),
                  pl.BlockSpec((1, k, n), lambda i: (i, 0, 0))],
        out_specs=pl.BlockSpec((m, n), lambda i: (0, i)),
        out_shape=jax.ShapeDtypeStruct((m, h * n), out_dtype),
        compiler_params=_cparams(1),
        name="head_matmul",
    )(a, w)


def _outproj_body(am_ref, yc_ref, oa_ref, w1_ref, w2_ref, w3_ref, res_ref, o_ref):
    acc = jnp.dot(am_ref[...], w1_ref[...], preferred_element_type=F32)
    acc += jnp.dot(yc_ref[...], w2_ref[...], preferred_element_type=F32)
    acc += jnp.dot(oa_ref[...], w3_ref[...], preferred_element_type=F32)
    o_ref[...] = res_ref[...] + acc


def _outproj(a_m, y_c, o_a, w_out, res):
    m, d = res.shape
    k1, k2, k3 = a_m.shape[1], y_c.shape[1], o_a.shape[1]
    assert k1 == k2 and k3 % k1 == 0
    tm = _pick(m, (512, 256, 128, 64, 32, 16, 8))
    tn = _pick(d, (512, 256, 128))
    return pl.pallas_call(
        _outproj_body,
        grid=(d // tn, m // tm),
        in_specs=[pl.BlockSpec((tm, k1), lambda j, i: (i, 0)),
                  pl.BlockSpec((tm, k2), lambda j, i: (i, 0)),
                  pl.BlockSpec((tm, k3), lambda j, i: (i, 0)),
                  pl.BlockSpec((k1, tn), lambda j, i: (0, j)),
                  pl.BlockSpec((k2, tn), lambda j, i: (1, j)),
                  pl.BlockSpec((k3, tn), lambda j, i: ((k1 + k2) // k3, j)),
                  pl.BlockSpec((tm, tn), lambda j, i: (i, j))],
        out_specs=pl.BlockSpec((tm, tn), lambda j, i: (i, j)),
        out_shape=jax.ShapeDtypeStruct((m, d), F32),
        compiler_params=_cparams(2),
        name="outproj",
    )(a_m, y_c, o_a, w_out, w_out, w_out, res)


def _split3(x):
    hi = x.astype(BF16)
    r1 = x - hi.astype(F32)
    mid = r1.astype(BF16)
    lo = (r1 - mid.astype(F32)).astype(BF16)
    return hi, mid, lo


def _dot_exact01(x, sel):
    hi, mid, lo = _split3(x)
    acc = jnp.dot(hi, sel, preferred_element_type=F32)
    acc += jnp.dot(mid, sel, preferred_element_type=F32)
    acc += jnp.dot(lo, sel, preferred_element_type=F32)
    return acc


def _dot01_left(sel, x):
    hi, mid, lo = _split3(x)
    acc = jnp.dot(sel, hi, preferred_element_type=F32)
    acc += jnp.dot(sel, mid, preferred_element_type=F32)
    acc += jnp.dot(sel, lo, preferred_element_type=F32)
    return acc


def _log_sigmoid(x):
    return jnp.minimum(x, 0.0) - jnp.log(1.0 + jnp.exp(-jnp.abs(x)))


def _mlstm_body(q_ref, k_ref, v_ref, o_ref, gt_ref, gb_ref, ng_ref, c0_ref, n0_ref, m0_ref,
                am_ref, cout_ref, nout_ref, mout_ref, c_sc, n_sc, m_sc,
                *, rows, heads, dh, t_valid, nchunk):
    LP = LANES
    c = pl.program_id(1)

    @pl.when(c == 0)
    def _():
        c_sc[...] = c0_ref[0]
        n_sc[...] = n0_ref[0]
        m_sc[...] = m0_ref[0]

    def pad_rows(x):
        if rows == LP:
            return x
        return jnp.concatenate([x, jnp.zeros((LP - rows, x.shape[1]), x.dtype)], axis=0)

    row_i = lax.broadcasted_iota(jnp.int32, (LP, LP), 0)
    col_i = lax.broadcasted_iota(jnp.int32, (LP, LP), 1)
    valid = ((c * rows + row_i) < t_valid) & (row_i < rows)
    x = pad_rows(gt_ref[...]) + gb_ref[...]
    ig = jnp.where(valid, x, NEG_INF)
    lf = jnp.where(valid, _log_sigmoid(x), 0.0)
    causal = col_i <= row_i
    tri = jnp.where(causal, 1.0, 0.0).astype(BF16)
    bc = _dot01_left(tri, lf)
    bt = bc.T
    igt = ig.T
    scale = dh ** -0.5
    for h in range(heads):
        hs = slice(h * dh, (h + 1) * dh)
        b_col = bc[:, heads + h:heads + h + 1]
        b_row = bt[heads + h:heads + h + 1, :]
        ig_row = igt[h:h + 1, :]
        ig_col = ig[:, h:h + 1]
        m_prev = m_sc[h:h + 1, 0:1]
        logd = jnp.where(causal, b_col - b_row + ig_row, NEG_INF)
        inter = b_col + m_prev
        m_t = jnp.maximum(inter, jnp.max(logd, axis=1, keepdims=True))
        qh = pad_rows(q_ref[:, hs])
        kh = pad_rows(k_ref[:, hs]) * scale
        vh = pad_rows(v_ref[:, hs])
        qb = qh.astype(BF16)
        kb = kh.astype(BF16)
        qk = lax.dot_general(qb, kb, (((1,), (1,)), ((), ())), preferred_element_type=F32)
        s = qk * jnp.exp(logd - m_t)
        w_inter = jnp.exp(inter - m_t)
        ch = c_sc[h]
        numer = (jnp.dot(s.astype(BF16), vh.astype(BF16), preferred_element_type=F32)
                 + w_inter * jnp.dot(qb, ch.astype(BF16), preferred_element_type=F32))
        nh = n_sc[h:h + 1, :]
        qn = jnp.sum(qh * nh, axis=1, keepdims=True)
        denom = jnp.maximum(jnp.abs(jnp.sum(s, axis=1, keepdims=True) + w_inter * qn), jnp.exp(-m_t))
        hh = numer / denom
        hn = hh * lax.rsqrt(jnp.mean(hh * hh, axis=1, keepdims=True) + EPS) * ng_ref[:, hs]
        gate = jax.nn.sigmoid(o_ref[:, hs])
        am_ref[:, hs] = (gate * hn[:rows]).astype(am_ref.dtype)
        m_new = m_t[LP - 1:LP, :]
        b_last = b_col[LP - 1:LP, :]
        wl = jnp.exp(b_last - b_col + ig_col - m_new)
        f_last = jnp.exp(b_last + m_prev - m_new)
        kv = lax.dot_general(kb, (wl * vh).astype(BF16), (((0,), (0,)), ((), ())),
                             preferred_element_type=F32)
        c_sc[h] = f_last * ch + kv
        n_sc[h:h + 1, :] = f_last * nh + jnp.sum(wl * kh, axis=0, keepdims=True)
        m_sc[h:h + 1, :] = jnp.broadcast_to(m_new, (1, LANES))

    @pl.when(c == nchunk - 1)
    def _():
        cout_ref[0] = c_sc[...]
        nout_ref[0] = n_sc[...]
        mout_ref[0] = m_sc[...]


def _mlstm(z, col, gate_bias, norm_g, c0, n0, m0, *, nseq, rows_per_seq, rows, t_valid, heads, dh):
    mw = heads * dh
    nchunk = rows_per_seq // rows
    m_rows = z.shape[0]
    qcol, kcol, vcol, ocol, gcol = (col[n] for n in ("q", "k", "v", "o", "gates"))

    def zspec(off, width):
        blk = off // width
        return pl.BlockSpec((rows, width), lambda b, c: (b * nchunk + c, blk))

    state3 = lambda: pl.BlockSpec((1, heads, dh, dh), lambda b, c: (b, 0, 0, 0))
    state2 = lambda: pl.BlockSpec((1, heads, LANES), lambda b, c: (b, 0, 0))
    body = functools.partial(_mlstm_body, rows=rows, heads=heads, dh=dh, t_valid=t_valid, nchunk=nchunk)
    return pl.pallas_call(
        body,
        grid=(nseq, nchunk),
        in_specs=[zspec(qcol, mw), zspec(kcol, mw), zspec(vcol, mw), zspec(ocol, mw), zspec(gcol, LANES),
                  pl.BlockSpec((1, LANES), lambda b, c: (0, 0)),
                  pl.BlockSpec((1, mw), lambda b, c: (0, 0)),
                  state3(), state2(), state2()],
        out_specs=[pl.BlockSpec((rows, mw), lambda b, c: (b * nchunk + c, 0)),
                   state3(), state2(), state2()],
        out_shape=[jax.ShapeDtypeStruct((m_rows, mw), BF16),
                   jax.ShapeDtypeStruct((nseq, heads, dh, dh), F32),
                   jax.ShapeDtypeStruct((nseq, heads, LANES), F32),
                   jax.ShapeDtypeStruct((nseq, heads, LANES), F32)],
        scratch_shapes=[pltpu.VMEM((heads, dh, dh), F32),
                        pltpu.VMEM((heads, LANES), F32),
                        pltpu.VMEM((heads, LANES), F32)],
        compiler_params=_cparams(2),
        name="mlstm",
    )(z, z, z, z, z, gate_bias, norm_g, c0, n0, m0)


CONV_HIST = 32


def _conv_body(a_ref, g_ref, hist_ref, w_ref, b_ref, lg_ref, lb_ref, y_ref, tail_ref, ubuf,
               *, rows, ksize, t_valid, nblk):
    c = pl.program_id(1)
    kh = ksize - 1
    lead = CONV_HIST - kh

    @pl.when(c == 0)
    def _():
        ubuf[0:CONV_HIST, :] = jnp.zeros((CONV_HIST, ubuf.shape[1]), F32)
        ubuf[lead:CONV_HIST, :] = hist_ref[0]

    u = a_ref[...] * jax.nn.sigmoid(g_ref[...])
    ubuf[CONV_HIST:CONV_HIST + rows, :] = u
    acc = jnp.broadcast_to(b_ref[...], u.shape)
    for k in range(ksize):
        acc = acc + w_ref[k:k + 1, :] * ubuf[lead + k:lead + k + rows, :]
    mu = jnp.mean(acc, axis=-1, keepdims=True)
    xc = acc - mu
    var = jnp.mean(xc * xc, axis=-1, keepdims=True)
    y = xc * lax.rsqrt(var + EPS) * lg_ref[...] + lb_ref[...]
    y_ref[...] = (y * jax.nn.sigmoid(y)).astype(y_ref.dtype)

    last_blk = (t_valid - 1) // rows
    off = CONV_HIST + (t_valid - last_blk * rows) - kh

    @pl.when(c == last_blk)
    def _():
        tail_ref[0] = ubuf[off:off + kh, :]

    if nblk > 1:
        ubuf[0:CONV_HIST, :] = ubuf[rows:rows + CONV_HIST, :]


def _conv(z, col, hist, w, b, lg, lb, *, nseq, rows_per_seq, rows, t_valid):
    ksize, cw = w.shape
    nblk = rows_per_seq // rows
    m_rows = z.shape[0]
    ablk, gblk = col["ca"] // cw, col["cg"] // cw
    vec = lambda: pl.BlockSpec((1, cw), lambda s, c: (0, 0))
    body = functools.partial(_conv_body, rows=rows, ksize=ksize, t_valid=t_valid, nblk=nblk)
    return pl.pallas_call(
        body,
        grid=(nseq, nblk),
        in_specs=[pl.BlockSpec((rows, cw), lambda s, c: (s * nblk + c, ablk)),
                  pl.BlockSpec((rows, cw), lambda s, c: (s * nblk + c, gblk)),
                  pl.BlockSpec((1, ksize - 1, cw), lambda s, c: (s, 0, 0)),
                  pl.BlockSpec((ksize, cw), lambda s, c: (0, 0)),
                  vec(), vec(), vec()],
        out_specs=[pl.BlockSpec((rows, cw), lambda s, c: (s * nblk + c, 0)),
                   pl.BlockSpec((1, ksize - 1, cw), lambda s, c: (s, 0, 0))],
        out_shape=[jax.ShapeDtypeStruct((m_rows, cw), BF16),
                   jax.ShapeDtypeStruct((nseq, ksize - 1, cw), F32)],
        scratch_shapes=[pltpu.VMEM((CONV_HIST + rows, cw), F32)],
        compiler_params=_cparams(2),
        name="conv",
    )(z, z, hist, w, b.reshape(1, cw), lg.reshape(1, cw), lb.reshape(1, cw))


def _kvprep_body(ckv_ref, kr_ref, g_ref, tab_ref, ckvn_ref, ckvb_ref, krd_ref, krb_ref):
    x = ckv_ref[...]
    y = x * lax.rsqrt(jnp.mean(x * x, axis=-1, keepdims=True) + EPS) * g_ref[...]
    ckvn_ref[...] = y
    ckvb_ref[...] = y.astype(BF16)
    t = kr_ref[...] * tab_ref[...]
    kr = t + pltpu.roll(t, LANES // 2, 1)
    krd_ref[...] = kr
    krb_ref[...] = kr.astype(BF16)


def _kvprep(z, col, g, tab, rows_per_tab):
    m = z.shape[0]
    r = g.shape[0]
    tm = _pick(rows_per_tab, (256, 128, 64, 32, 16, 8)) if rows_per_tab < m else _pick(m, (256, 128, 64, 32, 16, 8))
    ntab = tab.shape[0] // tm
    cblk, kblk = col["ckv"] // r, col["kr"] // LANES
    return pl.pallas_call(
        _kvprep_body,
        grid=(m // tm,),
        in_specs=[pl.BlockSpec((tm, r), lambda i: (i, cblk)),
                  pl.BlockSpec((tm, LANES), lambda i: (i, kblk)),
                  pl.BlockSpec((1, r), lambda i: (0, 0)),
                  pl.BlockSpec((tm, LANES), lambda i: (i % ntab, 0))],
        out_specs=[pl.BlockSpec((tm, r), lambda i: (i, 0)),
                   pl.BlockSpec((tm, r), lambda i: (i, 0)),
                   pl.BlockSpec((tm, LANES), lambda i: (i, 0)),
                   pl.BlockSpec((tm, LANES), lambda i: (i, 0))],
        out_shape=[jax.ShapeDtypeStruct((m, r), F32), jax.ShapeDtypeStruct((m, r), BF16),
                   jax.ShapeDtypeStruct((m, LANES), F32), jax.ShapeDtypeStruct((m, LANES), BF16)],
        compiler_params=_cparams(1),
        name="kv_prep",
    )(z, z, g.reshape(1, r), tab)


def _qprep_body(cq_ref, g_ref, w_ref, cos_ref, sin_ref, qn_ref, qr_ref, *, nope_w, rope_w, scale):
    x = cq_ref[...]
    y = (x * lax.rsqrt(jnp.mean(x * x, axis=-1, keepdims=True) + EPS) * g_ref[...]).astype(BF16)
    q = jnp.dot(y, w_ref[...], preferred_element_type=F32)
    qn_ref[...] = (q[:, :nope_w] * scale).astype(BF16)
    xr = q[:, nope_w:nope_w + rope_w]
    xrot = q[:, nope_w + rope_w:nope_w + 2 * rope_w]
    qr_ref[...] = ((xr * cos_ref[...] + xrot * sin_ref[...]) * scale).astype(BF16)


def _qprep(z, col, g, wq, cos_t, sin_t, *, nope_w, rope_w, scale):
    m = z.shape[0]
    r = g.shape[0]
    tm = _pick(min(m, cos_t.shape[0]), (256, 128, 64, 32, 16, 8))
    ntab = cos_t.shape[0] // tm
    cblk = col["cq"] // r
    body = functools.partial(_qprep_body, nope_w=nope_w, rope_w=rope_w, scale=scale)
    return pl.pallas_call(
        body,
        grid=(m // tm,),
        in_specs=[pl.BlockSpec((tm, r), lambda i: (i, cblk)),
                  pl.BlockSpec((1, r), lambda i: (0, 0)),
                  pl.BlockSpec(wq.shape, lambda i: (0, 0)),
                  pl.BlockSpec((tm, rope_w), lambda i: (i % ntab, 0)),
                  pl.BlockSpec((tm, rope_w), lambda i: (i % ntab, 0))],
        out_specs=[pl.BlockSpec((tm, nope_w), lambda i: (i, 0)),
                   pl.BlockSpec((tm, rope_w), lambda i: (i, 0))],
        out_shape=[jax.ShapeDtypeStruct((m, nope_w), BF16), jax.ShapeDtypeStruct((m, rope_w), BF16)],
        compiler_params=_cparams(1),
        name="q_prep",
    )(z, g.reshape(1, r), wq, cos_t, sin_t)


ATTN_BIG_CHUNK = 4 * LANES


def _attn_body(qn_ref, qr_ref, kn_ref, kr_ref, v_ref, o_ref, *, tq, tk, tbig):
    h = pl.program_id(1)
    qi = pl.program_id(2)
    lane = lax.broadcasted_iota(jnp.int32, (tq, LANES), 1)
    odd = (h % 2) == 1
    keep = (lane >= LANES // 2) == odd
    qr = jnp.where(keep, qr_ref[0].astype(F32), 0.0).astype(BF16)
    q = jnp.concatenate([qn_ref[0], qr], axis=1)
    def chunk(carry, start, width):
        m, l, acc = carry
        ks = pl.ds(pl.multiple_of(start, tk), width)
        row = lax.broadcasted_iota(jnp.int32, (tq, width), 0) + qi * tq
        col = lax.broadcasted_iota(jnp.int32, (tq, width), 1) + start
        k = jnp.concatenate([kn_ref[0, ks, :], kr_ref[0, ks, :]], axis=1)
        s = lax.dot_general(q, k, (((1,), (1,)), ((), ())), preferred_element_type=F32)
        s = jnp.where(col <= row, s, NEG_INF)
        m_new = jnp.maximum(m, jnp.max(s, axis=1, keepdims=True))
        p = jnp.exp(s - m_new)
        alpha = jnp.exp(m - m_new)
        l = alpha * l + jnp.sum(p, axis=1, keepdims=True)
        acc = alpha * acc + jnp.dot(p.astype(BF16), v_ref[0, ks, :], preferred_element_type=F32)
        return m_new, l, acc

    m0 = jnp.full((tq, 1), NEG_INF, F32)
    l0 = jnp.zeros((tq, 1), F32)
    a0 = jnp.zeros((tq, v_ref.shape[2]), F32)
    limit = qi * tq + tq
    nbig = limit // tbig
    nsmall = (limit - nbig * tbig) // tk
    carry = lax.fori_loop(0, nbig, lambda j, c: chunk(c, j * tbig, tbig), (m0, l0, a0))
    m, l, acc = lax.fori_loop(0, nsmall, lambda j, c: chunk(c, nbig * tbig + j * tk, tk), carry)
    o_ref[0] = (acc / l).astype(o_ref.dtype)


def _prompt_attention(qn, qr, kv, kr, *, nb, tp, heads):
    tq = tk = LANES
    nq = tp // tq
    dv = LANES
    qn3, qr3, kv3, kr3 = (a.reshape(nb, tp, a.shape[1]) for a in (qn, qr, kv, kr))
    body = functools.partial(_attn_body, tq=tq, tk=tk, tbig=ATTN_BIG_CHUNK)
    out = pl.pallas_call(
        body,
        grid=(nb, heads, nq),
        in_specs=[pl.BlockSpec((1, tq, LANES), lambda b, h, i: (b, i, h)),
                  pl.BlockSpec((1, tq, LANES), lambda b, h, i: (b, i, h // 2)),
                  pl.BlockSpec((1, tp, LANES), lambda b, h, i: (b, 0, h)),
                  pl.BlockSpec((1, tp, LANES), lambda b, h, i: (b, 0, 0)),
                  pl.BlockSpec((1, tp, dv), lambda b, h, i: (b, 0, heads + h))],
        out_specs=pl.BlockSpec((1, tq, dv), lambda b, h, i: (b, i, h)),
        out_shape=jax.ShapeDtypeStruct((nb, tp, heads * dv), BF16),
        compiler_params=_cparams(3),
        name="prompt_attention",
    )(qn3, qr3, kv3, kr3, kv3)
    return out.reshape(nb * tp, heads * dv)


PAGES_PER_STEP = 8


def _sattn_body(pt_ref, ql_ref, qr_ref, cnew_ref, knew_ref, *rest, npg, page, n_new, heads, nsteps):
    ck_refs = rest[:npg]
    kr_refs = rest[npg:2 * npg]
    o_ref = rest[2 * npg]
    m_sc, l_sc, acc_sc = rest[2 * npg + 1:]
    j = pl.program_id(1)
    rope = kr_refs[0].shape[-1]

    @pl.when(j == 0)
    def _():
        m_sc[...] = jnp.full(m_sc.shape, NEG_INF, F32)
        l_sc[...] = jnp.zeros(l_sc.shape, F32)
        acc_sc[...] = jnp.zeros(acc_sc.shape, F32)

    q = jnp.concatenate([ql_ref[0], qr_ref[0]], axis=1)

    def keycat(ck, kr):
        zpad = jnp.zeros((ck.shape[0], LANES - rope), F32)
        return jnp.concatenate([ck, kr, zpad], axis=1).astype(BF16)

    def fold(s, vals):
        m_old = m_sc[...]
        m_new = jnp.maximum(m_old, jnp.max(s, axis=1, keepdims=True))
        p = jnp.exp(s - m_new)
        alpha = jnp.exp(m_old - m_new)
        l_sc[...] = alpha * l_sc[...] + jnp.sum(p, axis=1, keepdims=True)
        acc_sc[...] = alpha * acc_sc[...] + jnp.dot(p.astype(BF16), vals, preferred_element_type=F32)
        m_sc[...] = m_new

    rank = ck_refs[0].shape[-1]
    k = jnp.concatenate([keycat(ck_refs[g][0, 0], kr_refs[g][0, 0]) for g in range(npg)], axis=0)
    s = lax.dot_general(q, k, (((1,), (1,)), ((), ())), preferred_element_type=F32)
    fold(s, k[:, :rank])

    @pl.when(j == nsteps - 1)
    def _():
        rows = q.shape[0]
        padr = page - cnew_ref.shape[1]
        cn = jnp.concatenate([cnew_ref[0], jnp.zeros((padr, cnew_ref.shape[2]), F32)], axis=0)
        kn = jnp.concatenate([knew_ref[0][:, :rope], jnp.zeros((padr, rope), F32)], axis=0)
        s = lax.dot_general(q, keycat(cn, kn), (((1,), (1,)), ((), ())), preferred_element_type=F32)
        qtok = lax.broadcasted_iota(jnp.int32, (rows, page), 0) // heads
        ktok = lax.broadcasted_iota(jnp.int32, (rows, page), 1)
        s = jnp.where((ktok <= qtok) & (ktok < n_new), s, NEG_INF)
        fold(s, cn.astype(BF16))
        o_ref[0] = (acc_sc[...] / l_sc[...]).astype(o_ref.dtype)


def _sample_attention(ql, qr, cnew, knew, cache_ckv, cache_krope, page_table, li, *, heads, n_new):
    nseq, rows, rank = ql.shape
    npages = page_table.shape[1]
    page = cache_ckv.shape[2]
    rope = cache_krope.shape[3]
    npg = _pick(npages, (PAGES_PER_STEP, 4, 2, 1))
    nsteps = npages // npg

    def cspec(g, width):
        return pl.BlockSpec((1, 1, page, width), lambda b, j, pt: (li, pt[b, j * npg + g], 0, 0))

    body = functools.partial(_sattn_body, npg=npg, page=page, n_new=n_new, heads=heads, nsteps=nsteps)
    grid_spec = pltpu.PrefetchScalarGridSpec(
        num_scalar_prefetch=1,
        grid=(nseq, nsteps),
        in_specs=[pl.BlockSpec((1, rows, rank), lambda b, j, pt: (b, 0, 0)),
                  pl.BlockSpec((1, rows, LANES), lambda b, j, pt: (b, 0, 0)),
                  pl.BlockSpec((1, SUBLANES, rank), lambda b, j, pt: (b, 0, 0)),
                  pl.BlockSpec((1, SUBLANES, LANES), lambda b, j, pt: (b, 0, 0))]
                 + [cspec(g, rank) for g in range(npg)] + [cspec(g, rope) for g in range(npg)],
        out_specs=pl.BlockSpec((1, rows, rank), lambda b, j, pt: (b, 0, 0)),
        scratch_shapes=[pltpu.VMEM((rows, 1), F32), pltpu.VMEM((rows, 1), F32), pltpu.VMEM((rows, rank), F32)],
    )
    return pl.pallas_call(
        body,
        grid_spec=grid_spec,
        out_shape=jax.ShapeDtypeStruct((nseq, rows, rank), BF16),
        compiler_params=_cparams(2),
        name="sample_attention",
    )(page_table, ql, qr, cnew, knew, *([cache_ckv] * npg), *([cache_krope] * npg))


def _top16(s, lane_f, width):
    vals, poss = [], []
    for _ in range(P_TOPK):
        m = jnp.max(s, axis=1, keepdims=True)
        pos = jnp.min(jnp.where(s == m, lane_f, float(width)), axis=1, keepdims=True)
        vals.append(m)
        poss.append(pos)
        s = jnp.where(lane_f == pos, NEG_INF, s)
    return vals, poss


def _route_body(q_ref, keys_ref, e0_ref, e1_ref, eidx_ref, gate_ref, *, nkeys):
    tm = q_ref.shape[0]
    h = pl.program_id(1)
    lane_f = lax.broadcasted_iota(jnp.int32, (tm, LANES), 1).astype(F32)
    lane2_f = lax.broadcasted_iota(jnp.int32, (tm, P_TOPK * P_TOPK), 1).astype(F32)
    lane_i = lax.broadcasted_iota(jnp.int32, (tm, LANES), 1)

    @pl.when(h == 0)
    def _():
        eidx_ref[...] = jnp.zeros(eidx_ref.shape, eidx_ref.dtype)
        gate_ref[...] = jnp.zeros(gate_ref.shape, gate_ref.dtype)

    packed = []
    for p in range(2):
        qj = q_ref[:, p * LANES:(p + 1) * LANES].astype(BF16)
        s = lax.dot_general(qj, keys_ref[p], (((1,), (1,)), ((), ())), preferred_element_type=F32)
        vals, poss = _top16(s, lane_f, nkeys)
        sv = jnp.zeros((tm, LANES), F32)
        si = jnp.zeros((tm, LANES), F32)
        for k in range(P_TOPK):
            sv = jnp.where(lane_i == k, vals[k], sv)
            si = jnp.where(lane_i == k, poss[k], si)
        packed.append((sv, si))
    (sv0, si0), (sv1, si1) = packed
    cand = _dot_exact01(sv0, e0_ref[...]) + _dot_exact01(sv1, e1_ref[...])
    cidx = (jnp.dot(si0.astype(BF16), e0_ref[...], preferred_element_type=F32) * float(nkeys)
            + jnp.dot(si1.astype(BF16), e1_ref[...], preferred_element_type=F32))
    tv, tpos = _top16(cand, lane2_f, P_TOPK * P_TOPK)
    denom = jnp.zeros((tm, 1), F32)
    eidx = jnp.zeros((tm, LANES), F32)
    ghead = jnp.zeros((tm, LANES), F32)
    for k in range(P_TOPK):
        e = jnp.sum(jnp.where(lane2_f == tpos[k], cidx, 0.0), axis=1, keepdims=True)
        w = jnp.exp(tv[k] - tv[0])
        denom = denom + w
        sel = lane_i == (h * P_TOPK + k)
        eidx = jnp.where(sel, e, eidx)
        ghead = jnp.where(sel, w, ghead)
    eidx_ref[...] += eidx.astype(jnp.int32)
    gate_ref[...] += ghead / denom


def _route(q, keys_b, *, pheads):
    m = q.shape[0]
    nkeys = keys_b.shape[1]
    tm = _pick(m, (256, 128, 64, 32, 16, 8))
    c = jnp.arange(P_TOPK * P_TOPK)
    e0 = (c[None, :] // P_TOPK == jnp.arange(LANES)[:, None]).astype(BF16)
    e1 = (c[None, :] % P_TOPK == jnp.arange(LANES)[:, None]).astype(BF16)
    body = functools.partial(_route_body, nkeys=nkeys)
    return pl.pallas_call(
        body,
        grid=(m // tm, pheads),
        in_specs=[pl.BlockSpec((tm, 2 * LANES), lambda i, h: (i, h)),
                  pl.BlockSpec(keys_b.shape, lambda i, h: (0, 0, 0)),
                  pl.BlockSpec(e0.shape, lambda i, h: (0, 0)),
                  pl.BlockSpec(e1.shape, lambda i, h: (0, 0))],
        out_specs=[pl.BlockSpec((tm, LANES), lambda i, h: (i, 0)),
                   pl.BlockSpec((tm, LANES), lambda i, h: (i, 0))],
        out_shape=[jax.ShapeDtypeStruct((m, LANES), jnp.int32), jax.ShapeDtypeStruct((m, LANES), F32)],
        compiler_params=_cparams(2),
        name="peer_route",
    )(q, keys_b, e0, e1)


def _tree_sum(parts):
    while len(parts) > 1:
        parts = [parts[i] + parts[i + 1] for i in range(0, len(parts) - 1, 2)] + (
            [parts[-1]] if len(parts) % 2 else [])
    return parts[0]


def _peer_body(eidx_ref, gate_ref, x_ref, h_ref, uv_hbm, o_ref, buf, xb_sc, w_sc, sem,
               *, li, n_tok, nsel):
    d = x_ref.shape[-1]
    nchunk = nsel // SUBLANES
    nlane = d // LANES
    half = d // 2
    per_hid, per_out = SUBLANES // 2, SUBLANES // 4

    def read_ids(t, i, k0, n):
        return [eidx_ref[0, t, i * SUBLANES + k] for k in range(k0, k0 + n)]

    def start_ids(ids, i, k0, slot):
        for k, e in enumerate(ids, start=k0):
            pltpu.make_async_copy(uv_hbm.at[li, pl.ds(e, 1), :], buf.at[slot, i, pl.ds(k, 1), :],
                                  sem.at[slot]).start()

    def wait(slot):
        pltpu.make_async_copy(buf.at[slot], buf.at[slot], sem.at[slot]).wait()

    eye = (lax.broadcasted_iota(jnp.int32, (nsel, LANES), 0)
           == lax.broadcasted_iota(jnp.int32, (nsel, LANES), 1))

    def token(t, slot, prefetch):
        wait(slot)
        xb_sc[...] = jnp.broadcast_to(x_ref[0, pl.ds(t, 1), :], (SUBLANES, d))
        grow = gate_ref[0, pl.ds(t, 1), :]
        gcol = jnp.sum(jnp.where(eye, grow, 0.0), axis=1, keepdims=True)

        def hid_step(i, c):
            rs = pl.ds(pl.multiple_of(i * SUBLANES, SUBLANES), SUBLANES)
            ids = read_ids(t + 1, i, 0, per_hid) if prefetch else ()
            parts = [buf[slot, i, :, j * LANES:(j + 1) * LANES] * xb_sc[:, j * LANES:(j + 1) * LANES]
                     for j in range(nlane)]
            w_sc[rs, :] = _tree_sum(parts)
            start_ids(ids, i, 0, 1 - slot)
            return c
        lax.fori_loop(0, nchunk, hid_step, 0)
        hid = jnp.sum(w_sc[...], axis=1, keepdims=True)
        w_sc[...] = jnp.broadcast_to(jax.nn.gelu(hid) * gcol, (nsel, LANES))

        for hf in range(2):
            def out_step(i, acc, hf=hf):
                rs = pl.ds(pl.multiple_of(i * SUBLANES, SUBLANES), SUBLANES)
                k0 = per_hid + hf * per_out
                ids = read_ids(t + 1, i, k0, per_out) if prefetch else ()
                w8 = jnp.tile(w_sc[rs, :], (1, half // LANES))
                acc = acc + buf[slot, i, :, d + hf * half:d + (hf + 1) * half] * w8
                start_ids(ids, i, k0, 1 - slot)
                return acc
            acc = lax.fori_loop(0, nchunk, out_step, jnp.zeros((SUBLANES, half), F32))
            cols = slice(hf * half, (hf + 1) * half)
            o_ref[0, pl.ds(t, 1), cols] = h_ref[0, pl.ds(t, 1), cols] + jnp.sum(acc, axis=0, keepdims=True)

    o_ref[...] = h_ref[...]

    def first_rows(i, c):
        start_ids(read_ids(0, i, 0, SUBLANES), i, 0, 0)
        return c
    lax.fori_loop(0, nchunk, first_rows, 0)

    def token_step(t, c):
        token(t, t % 2, True)
        return c
    lax.fori_loop(0, n_tok - 1, token_step, 0)
    token(n_tok - 1, (n_tok - 1) % 2, False)


def _peer_experts(eidx, gate, xn, hres, peer_uv, li, *, nseq, rows_per_seq, rows, n_blk, n_tok):
    d = xn.shape[1]
    nsel = eidx.shape[1]
    e3, g3 = eidx.reshape(nseq, rows_per_seq, nsel), gate.reshape(nseq, rows_per_seq, nsel)
    x3, h3 = xn.reshape(nseq, rows_per_seq, d), hres.reshape(nseq, rows_per_seq, d)
    body = functools.partial(_peer_body, li=li, n_tok=n_tok, nsel=nsel)
    out = pl.pallas_call(
        body,
        grid=(nseq, n_blk),
        in_specs=[pl.BlockSpec((1, rows, nsel), lambda s, c: (s, c, 0), memory_space=pltpu.SMEM),
                  pl.BlockSpec((1, rows, nsel), lambda s, c: (s, c, 0)),
                  pl.BlockSpec((1, rows, d), lambda s, c: (s, c, 0)),
                  pl.BlockSpec((1, rows, d), lambda s, c: (s, c, 0)),
                  pl.BlockSpec(memory_space=pl.ANY)],
        out_specs=pl.BlockSpec((1, rows, d), lambda s, c: (s, c, 0)),
        out_shape=jax.ShapeDtypeStruct((nseq, rows_per_seq, d), F32),
        scratch_shapes=[pltpu.VMEM((2, nsel // SUBLANES, SUBLANES, 2 * d), F32), pltpu.VMEM((SUBLANES, d), F32),
                        pltpu.VMEM((nsel, LANES), F32), pltpu.SemaphoreType.DMA((2,))],
        input_output_aliases={3: 0},
        compiler_params=_cparams(2),
        name="peer_experts",
    )(e3, g3, x3, h3, peer_uv)
    return out.reshape(nseq * rows_per_seq, d)


def _layout(mw, cw, kvr, qr):
    col, off = {}, 0
    for name, width in (("q", mw), ("k", mw), ("v", mw), ("o", mw), ("ca", cw), ("cg", cw),
                        ("ckv", kvr), ("kr", LANES), ("gates", LANES), ("cq", qr)):
        off = -(-off // width) * width
        col[name] = off
        off += width
    total = -(-off // LANES) * LANES
    return col, total


def _rot_half(w):
    half = w.shape[-1] // 2
    return jnp.concatenate([-w[..., half:], w[..., :half]], axis=-1)


def _rope_tables(pos, rope, heads):
    inv = jnp.power(ROPE_THETA, -jnp.arange(0, rope, 2, dtype=F32) / rope)
    ang = pos.astype(F32)[:, None] * inv[None, :]
    cos, sin = jnp.cos(ang), jnp.sin(ang)
    cos2, sin2 = jnp.concatenate([cos, cos], -1), jnp.concatenate([sin, sin], -1)
    ktab = jnp.concatenate([cos2, sin2], -1)
    return ktab, jnp.tile(cos2, (1, heads)), jnp.tile(sin2, (1, heads))


def kernel(x_prompt, x_sample, cache_ckv, cache_krope, state_mlstm_C, state_mlstm_n, state_mlstm_m,
           state_conv, page_table, meta_tokens, ln_mix_g, w_in, gate_b, mlstm_norm_g, conv_w, conv_b,
           conv_ln_g, conv_ln_b, q_norm_g, kv_norm_g, w_uq, w_uk, w_uv, w_out, ln_ffn_g, peer_wq,
           peer_keys, peer_u, peer_v, final_g):
    nb, seq, d = x_prompt.shape
    db, dseq, _ = x_sample.shape
    depth = w_in.shape[0]
    n_meta = meta_tokens.shape[0]
    mh, dh = state_mlstm_C.shape[2], state_mlstm_C.shape[3]
    mw = mh * dh
    ksize, cw = conv_w.shape[1], conv_w.shape[2]
    qrank, kvr = q_norm_g.shape[1], kv_norm_g.shape[1]
    ah, nope = w_uk.shape[2], w_uk.shape[3]
    rope = w_uq.shape[3] - nope
    av = w_uv.shape[3]
    page = cache_ckv.shape[2]
    past = page_table.shape[1] * page
    nkeys, phalf = peer_keys.shape[2], peer_keys.shape[3]
    pheads = peer_wq.shape[2] // (2 * phalf)
    assert dh == LANES and nope == LANES and av == LANES and 2 * rope == LANES and phalf == LANES
    assert nkeys == LANES and pheads * P_TOPK == LANES and ah % 2 == 0 and dseq <= SUBLANES
    assert mw == cw and (ah * av) % mw == 0 and ksize - 1 <= CONV_HIST
    attn_scale = (nope + rope) ** -0.5

    t_valid = n_meta + seq
    tp = -(-t_valid // LANES) * LANES
    srows = SUBLANES
    col, n_in = _layout(mw, cw, kvr, qrank)
    in_sizes = (mw, mw, mw, mw, mh, mh, cw, cw, qrank, kvr, rope)
    in_off = [0]
    for s in in_sizes:
        in_off.append(in_off[-1] + s)

    meta = jnp.broadcast_to(meta_tokens[None].astype(F32), (nb, n_meta, d))
    h_p = jnp.concatenate([meta, x_prompt, jnp.zeros((nb, tp - t_valid, d), F32)], axis=1).reshape(nb * tp, d)
    h_s = jnp.pad(x_sample, ((0, 0), (0, srows - dseq), (0, 0))).reshape(db * srows, d)

    ktab_p, cos_p, sin_p = _rope_tables(jnp.arange(tp), rope, ah)
    pos_s = past + jnp.arange(srows)
    stile = _pick(db * srows, (256, 128, 64, 32, 16, 8)) // srows
    ktab_s, cos_s, sin_s = (jnp.tile(t, (stile, 1)) for t in _rope_tables(pos_s, rope, ah))

    peer_rows = _pick(t_valid, (48, 64, 32, 16, 8))
    assert tp % SUBLANES == 0

    groups = (
        dict(h=h_p, nseq=nb, rps=tp, tval=t_valid, mrows=LANES, crows=LANES, ktab=ktab_p, cos=cos_p, sin=sin_p,
             tabrows=tp, prow=peer_rows, pblk=t_valid // peer_rows, ptok=peer_rows),
        dict(h=h_s, nseq=db, rps=srows, tval=dseq, mrows=srows, crows=srows, ktab=ktab_s, cos=cos_s, sin=sin_s,
             tabrows=stile * srows, prow=srows, pblk=1, ptok=dseq),
    )
    outs = [[[] for _ in range(6)] for _ in range(2)]
    peer_uv = jnp.concatenate([peer_u, peer_v], axis=-1)

    for li in range(depth):
        w = w_in[li]
        seg = {n: w[:, in_off[i]:in_off[i + 1]] for i, n in enumerate(
            ("q", "k", "v", "o", "ig", "fg", "ca", "cg", "cq", "ckv", "kr"))}
        pieces = {"q": seg["q"], "k": seg["k"], "v": seg["v"], "o": seg["o"], "ca": seg["ca"], "cg": seg["cg"],
                  "ckv": seg["ckv"], "kr": jnp.concatenate([seg["kr"], _rot_half(seg["kr"])], axis=1),
                  "gates": jnp.concatenate([seg["ig"], seg["fg"]], axis=1), "cq": seg["cq"]}
        w_packed = jnp.zeros((d, n_in), BF16)
        for name, piece in pieces.items():
            w_packed = lax.dynamic_update_slice(w_packed, piece.astype(BF16), (0, col[name]))
        gbias = jnp.zeros((1, LANES), F32).at[0, :2 * mh].set(gate_b[li])
        wq_n = w_uq[li][:, :, :nope].reshape(qrank, ah * nope)
        wq_r = w_uq[li][:, :, nope:]
        wq_packed = jnp.concatenate([wq_n, wq_r.reshape(qrank, ah * rope),
                                     _rot_half(wq_r).reshape(qrank, ah * rope)], axis=1).astype(BF16)
        wkv = jnp.concatenate([w_uk[li].reshape(kvr, ah * nope), w_uv[li].reshape(kvr, ah * av)], axis=1).astype(BF16)
        w_ukt = jnp.transpose(w_uk[li], (1, 2, 0)).astype(BF16)
        w_uvh = jnp.transpose(w_uv[li], (1, 0, 2)).astype(BF16)
        w_out_b = w_out[li].astype(BF16)
        wpq = peer_wq[li].astype(BF16)
        keys_b = peer_keys[li].astype(BF16)

        for gi, g in enumerate(groups):
            h = g["h"]
            nseq, rps = g["nseq"], g["rps"]
            (xn,) = _rmsnorm(h, ln_mix_g[li], (BF16,))
            z = _matmul(xn, w_packed, F32)

            if gi == 0:
                c0 = jnp.zeros((nseq, mh, dh, dh), F32)
                n0 = jnp.zeros((nseq, mh, LANES), F32)
                m0 = jnp.zeros((nseq, mh, LANES), F32)
                hist = jnp.zeros((nseq, ksize - 1, cw), F32)
            else:
                c0 = state_mlstm_C[li].astype(F32)
                n0 = state_mlstm_n[li].astype(F32)
                m0 = jnp.broadcast_to(state_mlstm_m[li].astype(F32)[:, :, None], (nseq, mh, LANES))
                hist = state_conv[li].astype(F32)
            a_m, c_new, n_new, m_new = _mlstm(z, col, gbias, mlstm_norm_g[li].reshape(1, mw), c0, n0, m0,
                                              nseq=nseq, rows_per_seq=rps, rows=g["mrows"], t_valid=g["tval"],
                                              heads=mh, dh=dh)
            y_c, tail = _conv(z, col, hist, conv_w[li], conv_b[li], conv_ln_g[li], conv_ln_b[li],
                              nseq=nseq, rows_per_seq=rps, rows=g["crows"], t_valid=g["tval"])
            ckv_n, ckv_b, kr_d, kr_b = _kvprep(z, col, kv_norm_g[li], g["ktab"], g["tabrows"])
            qn, qr = _qprep(z, col, q_norm_g[li], wq_packed, g["cos"], g["sin"],
                            nope_w=ah * nope, rope_w=ah * rope, scale=attn_scale)
            if gi == 0:
                kv = _matmul(ckv_b, wkv, BF16)
                o_a = _prompt_attention(qn, qr, kv, kr_b, nb=nseq, tp=rps, heads=ah)
            else:
                rows = srows * ah
                q_lat = _head_matmul(qn, w_ukt, BF16).reshape(nseq, rows, kvr)
                q_rope = jnp.pad(qr.reshape(nseq, rows, rope), ((0, 0), (0, 0), (0, LANES - rope)))
                o_lat = _sample_attention(q_lat, q_rope, ckv_n.reshape(nseq, srows, kvr),
                                          kr_d.reshape(nseq, srows, LANES), cache_ckv, cache_krope,
                                          page_table, li, heads=ah, n_new=dseq)
                o_a = _head_matmul(o_lat.reshape(nseq * srows, ah * kvr), w_uvh, BF16)
            h = _outproj(a_m, y_c, o_a, w_out_b, h)

            xf, xb = _rmsnorm(h, ln_ffn_g[li], (F32, BF16))
            pq = _matmul(xb, wpq, F32)
            eidx, gate = _route(pq, keys_b, pheads=pheads)
            h = _peer_experts(eidx, gate, xf, h, peer_uv, li, nseq=nseq, rows_per_seq=rps,
                              rows=g["prow"], n_blk=g["pblk"], n_tok=g["ptok"])
            g["h"] = h

            tv = g["tval"]
            vals = (ckv_n.reshape(nseq, rps, kvr)[:, :tv], kr_d.reshape(nseq, rps, LANES)[:, :tv, :rope],
                    c_new, n_new[:, :, :dh], m_new[:, :, 0], tail)
            for lst, val in zip(outs[gi], vals):
                lst.append(val)

    (y_p,) = _rmsnorm(groups[0]["h"], final_g, (F32,))
    (y_s,) = _rmsnorm(groups[1]["h"], final_g, (F32,))
    y_prompt = y_p.reshape(nb, tp, d)[:, n_meta:t_valid]
    y_sample = y_s.reshape(db, srows, d)[:, :dseq]
    stacked_p = [jnp.stack(v) for v in outs[0]]
    stacked_s = [jnp.stack(v) for v in outs[1]]
    return (y_prompt, y_sample, *stacked_p, *stacked_s)
```
